```python
import jax, jax.numpy as jnp
from jax import lax
import numpy as np

D_MODEL = 1024
BATCH = 8
SEQ = 4096
DEPTH = 2

D_MIX = D_MODEL
GMLP_HEADS = 8
FNET_GROUPS = 8
GMLP_WIDTH = D_MIX // 2
FNET_WIDTH = D_MIX - GMLP_WIDTH
GMLP_HEAD_DIM = GMLP_WIDTH // GMLP_HEADS
FNET_GROUP_DIM = FNET_WIDTH // FNET_GROUPS
CHUNK = 128
IN_COLS = 2 * GMLP_WIDTH + FNET_WIDTH
PEER_HEADS = 8
PEER_KEYS = 128
PEER_EXPERTS = PEER_KEYS * PEER_KEYS
PEER_TOPK = 16
PEER_QDIM = 256
PEER_HALF = PEER_QDIM // 2
PEER_BLOCK = 128
N_MOD = 6
EPS = 1e-6

kernel_name = 'hybrid_gmlp_fnet_peer_adaln'


def rms_norm(x, g):
    xf = x.astype(jnp.float32)
    y = xf * lax.rsqrt(jnp.mean(xf * xf, axis=-1, keepdims=True) + EPS)
    return (y * g.astype(jnp.float32)).astype(x.dtype)


def layer_norm_noaffine(x):
    xf = x.astype(jnp.float32)
    mu = jnp.mean(xf, axis=-1, keepdims=True)
    var = jnp.mean(jnp.square(xf - mu), axis=-1, keepdims=True)
    return ((xf - mu) * lax.rsqrt(var + EPS)).astype(x.dtype)


def modulate(h, shift, scale):
    return h * (1 + scale[:, None, :]) + shift[:, None, :]


def hybrid_mixer(h, w_in, ws, bs, fw, fb, gain_a, gain_b, w_out):
    B, S, _ = h.shape
    z = h @ w_in
    zu = z[..., :GMLP_WIDTH]
    zv = z[..., GMLP_WIDTH:2 * GMLP_WIDTH]
    zf = z[..., 2 * GMLP_WIDTH:]
    u = jax.nn.gelu(zu, approximate=False).reshape(B, S, GMLP_HEADS, GMLP_HEAD_DIM)
    v = layer_norm_noaffine(jax.nn.gelu(zv, approximate=False).reshape(B, S, GMLP_HEADS, GMLP_HEAD_DIM))
    v = v.reshape(B, S // CHUNK, CHUNK, GMLP_HEADS, GMLP_HEAD_DIM)
    sv = jnp.einsum('hpq,bcqhd->bcphd', ws, v) + bs.T[None, None, :, :, None]
    y_a = (u * sv.reshape(B, S, GMLP_HEADS, GMLP_HEAD_DIM)).reshape(B, S, GMLP_WIDTH)
    f = zf.reshape(B, S, FNET_GROUPS, FNET_GROUP_DIM).astype(jnp.float32)
    fr = jnp.fft.fft2(f, axes=(1, 3), norm='ortho').real.astype(h.dtype)
    y_b = (jnp.einsum('bsgd,gde->bsge', fr, fw) + fb[None, None]).reshape(B, S, FNET_WIDTH)
    y = jnp.concatenate([rms_norm(y_a, gain_a), rms_norm(y_b, gain_b)], axis=-1)
    return y @ w_out


def peer_ffn(h, wq, k1, k2, U, V):
    B, S, D = h.shape
    hb = h.reshape(B * S // PEER_BLOCK, PEER_BLOCK, D)

    def block(hc):
        q = (hc @ wq).reshape(PEER_BLOCK, PEER_HEADS, 2, PEER_HALF)
        s1 = jnp.einsum('thd,hnd->thn', q[:, :, 0], k1).astype(jnp.float32)
        s2 = jnp.einsum('thd,hnd->thn', q[:, :, 1], k2).astype(jnp.float32)
        v1, i1 = lax.top_k(s1, PEER_TOPK)
        v2, i2 = lax.top_k(s2, PEER_TOPK)
        cand = (v1[..., :, None] + v2[..., None, :]).reshape(PEER_BLOCK, PEER_HEADS, PEER_TOPK * PEER_TOPK)
        cidx = (i1[..., :, None] * PEER_KEYS + i2[..., None, :]).reshape(PEER_BLOCK, PEER_HEADS, PEER_TOPK * PEER_TOPK)
        sc, pos = lax.top_k(cand, PEER_TOPK)
        eidx = jnp.take_along_axis(cidx, pos, axis=-1).reshape(PEER_BLOCK, PEER_HEADS * PEER_TOPK)
        g = jax.nn.softmax(sc, axis=-1).reshape(PEER_BLOCK, PEER_HEADS * PEER_TOPK).astype(hc.dtype)
        u_sel = jnp.take(U, eidx, axis=0)
        a = jax.nn.gelu(jnp.einsum('td,tkd->tk', hc, u_sel), approximate=False)
        v_sel = jnp.take(V, eidx, axis=0)
        return jnp.einsum('tk,tkd->td', g * a, v_sel)

    return lax.map(block, hb).reshape(B, S, D)


def setup_inputs(seed: int = 0) -> dict:
    key = jax.random.key(seed)
    ks = jax.random.split(key, 22)

    def nrm(k, shape, scale):
        return jax.random.normal(k, shape, jnp.float32) * scale

    L, D = DEPTH, D_MODEL
    return {
        'x': nrm(ks[0], (BATCH, SEQ, D), 1.0),
        'c': nrm(ks[1], (BATCH, D), 1.0),
        'mod_w': nrm(ks[2], (L, D, N_MOD * D), D ** -0.5),
        'mod_b': nrm(ks[3], (L, N_MOD * D), 0.02),
        'norm1_g': 1.0 + nrm(ks[4], (L, D), 0.02),
        'w_in': nrm(ks[5], (L, D, IN_COLS), D ** -0.5),
        'gmlp_ws': nrm(ks[6], (L, GMLP_HEADS, CHUNK, CHUNK), CHUNK ** -0.5),
        'gmlp_bs': 1.0 + nrm(ks[7], (L, GMLP_HEADS, CHUNK), 0.02),
        'fnet_w': nrm(ks[8], (L, FNET_GROUPS, FNET_GROUP_DIM, FNET_GROUP_DIM), FNET_GROUP_DIM ** -0.5),
        'fnet_b': nrm(ks[9], (L, FNET_GROUPS, FNET_GROUP_DIM), 0.02),
        'gain_a': 1.0 + nrm(ks[10], (L, GMLP_WIDTH), 0.02),
        'gain_b': 1.0 + nrm(ks[11], (L, FNET_WIDTH), 0.02),
        'w_out': nrm(ks[12], (L, D_MIX, D), D_MIX ** -0.5),
        'norm2_g': 1.0 + nrm(ks[13], (L, D), 0.02),
        'peer_wq': nrm(ks[14], (L, D, PEER_HEADS * PEER_QDIM), D ** -0.5),
        'peer_k1': nrm(ks[15], (L, PEER_HEADS, PEER_KEYS, PEER_HALF), PEER_HALF ** -0.5),
        'peer_k2': nrm(ks[16], (L, PEER_HEADS, PEER_KEYS, PEER_HALF), PEER_HALF ** -0.5),
        'peer_u': nrm(ks[17], (L, PEER_EXPERTS, D), D ** -0.5),
        'peer_v': nrm(ks[18], (L, PEER_EXPERTS, D), PEER_HEADS ** -0.5),
        'final_g': 1.0 + nrm(ks[19], (D,), 0.02),
    }


def reference(x, c, mod_w, mod_b, norm1_g, w_in, gmlp_ws, gmlp_bs, fnet_w, fnet_b,
              gain_a, gain_b, w_out, norm2_g, peer_wq, peer_k1, peer_k2, peer_u, peer_v, final_g):
    sc_in = jax.nn.silu(c)
    for l in range(DEPTH):
        mod = sc_in @ mod_w[l] + mod_b[l]
        sh1, scl1, g1, sh2, scl2, g2 = jnp.split(mod, N_MOD, axis=-1)
        h = modulate(rms_norm(x, norm1_g[l]), sh1, scl1)
        x = x + g1[:, None, :] * hybrid_mixer(h, w_in[l], gmlp_ws[l], gmlp_bs[l], fnet_w[l], fnet_b[l],
                                              gain_a[l], gain_b[l], w_out[l])
        h = modulate(rms_norm(x, norm2_g[l]), sh2, scl2)
        x = x + g2[:, None, :] * peer_ffn(h, peer_wq[l], peer_k1[l], peer_k2[l], peer_u[l], peer_v[l])
    return rms_norm(x, final_g)
```

```python
import functools
import math

import numpy as np
import jax
import jax.numpy as jnp
from jax import lax
from jax.experimental import pallas as pl
from jax.experimental.pallas import tpu as pltpu

GMLP_HEADS = 8
FNET_GROUPS = 8
CHUNK = 128
PEER_HEADS = 8
PEER_KEYS = 128
PEER_TOPK = 16
N_MOD = 6
EPS = 1e-6

VMEM_LIMIT_BYTES = 56 * 1024 * 1024

F32 = jnp.float32
BF16 = jnp.bfloat16
NEG_INF = float("-inf")


def _params(*sem):
    return pltpu.CompilerParams(dimension_semantics=sem, vmem_limit_bytes=VMEM_LIMIT_BYTES)


def _gelu(x):
    return 0.5 * x * (1.0 + lax.erf(x * (1.0 / math.sqrt(2.0))))


def _rms(x, g):
    return x * lax.rsqrt(jnp.mean(x * x, axis=-1, keepdims=True) + EPS) * g


def _mod_kernel(c_ref, w_ref, b_ref, o_ref):
    c = c_ref[...]
    sc = c * jax.nn.sigmoid(c)
    o_ref[0] = jnp.dot(sc, w_ref[0], preferred_element_type=F32) + b_ref[0]


def _mod_call(c, mod_w, mod_b):
    L, D, M = mod_w.shape
    B = c.shape[0]
    tn = 1536
    return pl.pallas_call(
        _mod_kernel,
        grid=(L, M // tn),
        in_specs=[
            pl.BlockSpec((B, D), lambda l, j: (0, 0)),
            pl.BlockSpec((1, D, tn), lambda l, j: (l, 0, j)),
            pl.BlockSpec((1, 1, tn), lambda l, j: (l, 0, j)),
        ],
        out_specs=pl.BlockSpec((1, B, tn), lambda l, j: (l, 0, j)),
        out_shape=jax.ShapeDtypeStruct((L, B, M), F32),
        compiler_params=_params("parallel", "parallel"),
        name="adaln_mod",
    )(c, mod_w, mod_b.reshape(L, 1, M))


def _mixer_in_kernel(x_ref, mod_ref, g_ref, win_ref, avg_ref, ws_ref, bs_ref, hmask_ref, cd_ref, sd_ref,
                     ya_ref, fc_ref, fs_ref):
    ts = x_ref.shape[1]
    wa = ya_ref.shape[2]
    x = x_ref[0]
    h = _rms(x, g_ref[...]) * (1.0 + mod_ref[0, 1:2, :]) + mod_ref[0, 0:1, :]
    z = jnp.dot(h.astype(BF16), win_ref[...], preferred_element_type=F32)
    u = _gelu(z[:, :wa])
    gv = _gelu(z[:, wa:2 * wa])
    zf = z[:, 2 * wa:].astype(BF16)
    mu = jnp.dot(gv.astype(BF16), avg_ref[...], preferred_element_type=F32)
    dv = gv - mu
    var = jnp.dot((dv * dv).astype(BF16), avg_ref[...], preferred_element_type=F32)
    v = (dv * lax.rsqrt(var + EPS)).astype(BF16)
    hmask = hmask_ref[...]
    for ci in range(ts // CHUNK):
        vc = v[ci * CHUNK:(ci + 1) * CHUNK, :]
        vstack = jnp.concatenate([vc] * GMLP_HEADS, axis=0) * hmask
        sv = jnp.dot(ws_ref[...], vstack, preferred_element_type=F32) + bs_ref[...]
        ya_ref[0, ci * CHUNK:(ci + 1) * CHUNK, :] = u[ci * CHUNK:(ci + 1) * CHUNK, :] * sv
    fc_ref[...] = jnp.dot(zf, cd_ref[...], preferred_element_type=F32).astype(BF16)
    fs_ref[...] = jnp.dot(zf, sd_ref[...], preferred_element_type=F32).astype(BF16)


def _mixer_in_call(x, mod, g, win, avg, ws_cat, bs_full, hmask, cd, sd, ts):
    B, S, D = x.shape
    wa = avg.shape[0]
    wf = cd.shape[0]
    const = lambda shape: pl.BlockSpec(shape, lambda b, s: (0,) * len(shape))
    return pl.pallas_call(
        _mixer_in_kernel,
        grid=(B, S // ts),
        in_specs=[
            pl.BlockSpec((1, ts, D), lambda b, s: (b, s, 0)),
            pl.BlockSpec((1, N_MOD, D), lambda b, s: (b, 0, 0)),
            const(g.shape), const(win.shape), const(avg.shape), const(ws_cat.shape),
            const(bs_full.shape), const(hmask.shape), const(cd.shape), const(sd.shape),
        ],
        out_specs=[
            pl.BlockSpec((1, ts, wa), lambda b, s: (b, s, 0)),
            pl.BlockSpec((ts, wf), lambda b, s: (s, b)),
            pl.BlockSpec((ts, wf), lambda b, s: (s, b)),
        ],
        out_shape=[
            jax.ShapeDtypeStruct((B, S, wa), F32),
            jax.ShapeDtypeStruct((S, B * wf), BF16),
            jax.ShapeDtypeStruct((S, B * wf), BF16),
        ],
        compiler_params=_params("parallel", "parallel"),
        name="mixer_in",
    )(x, mod, g, win, avg, ws_cat, bs_full, hmask, cd, sd)


def _seq_dft_kernel(wc_ref, wsn_ref, fc_ref, fs_ref, o_ref):
    acc = jnp.dot(wc_ref[...], fc_ref[...], preferred_element_type=F32)
    acc += jnp.dot(wsn_ref[...], fs_ref[...], preferred_element_type=F32)
    o_ref[...] = acc.astype(o_ref.dtype)


def _seq_dft_call(wc, wsn, fc, fs, tm, tn):
    S = wc.shape[0]
    ncol = fc.shape[1]
    return pl.pallas_call(
        _seq_dft_kernel,
        grid=(S // tm, ncol // tn),
        in_specs=[
            pl.BlockSpec((tm, S), lambda i, j: (i, 0)),
            pl.BlockSpec((tm, S), lambda i, j: (i, 0)),
            pl.BlockSpec((S, tn), lambda i, j: (0, j)),
            pl.BlockSpec((S, tn), lambda i, j: (0, j)),
        ],
        out_specs=pl.BlockSpec((tm, tn), lambda i, j: (i, j)),
        out_shape=jax.ShapeDtypeStruct((S, ncol), BF16),
        compiler_params=_params("parallel", "parallel"),
        name="seq_dft",
    )(wc, wsn, fc, fs)


def _mixer_out_kernel(x_ref, ya_ref, fr_ref, mod_ref, fw_ref, fb_ref, ga_ref, gb_ref, wout_ref, g2_ref,
                      x1_ref, h2_ref):
    wa = ya_ref.shape[2]
    yb = jnp.dot(fr_ref[...], fw_ref[...], preferred_element_type=F32) + fb_ref[...]
    na = _rms(ya_ref[0], ga_ref[...]).astype(BF16)
    nb = _rms(yb, gb_ref[...]).astype(BF16)
    mix = jnp.dot(na, wout_ref[:wa, :], preferred_element_type=F32)
    mix += jnp.dot(nb, wout_ref[wa:, :], preferred_element_type=F32)
    x1 = x_ref[0] + mod_ref[0, 2:3, :] * mix
    x1_ref[0] = x1
    h2 = _rms(x1, g2_ref[...]) * (1.0 + mod_ref[0, 4:5, :]) + mod_ref[0, 3:4, :]
    h2_ref[0] = h2.astype(BF16)


def _mixer_out_call(x, ya, fr, mod, fw_bd, fb, ga, gb, wout, g2, ts):
    B, S, D = x.shape
    wa = ya.shape[2]
    wf = fw_bd.shape[0]
    const = lambda shape: pl.BlockSpec(shape, lambda b, s: (0,) * len(shape))
    return pl.pallas_call(
        _mixer_out_kernel,
        grid=(B, S // ts),
        in_specs=[
            pl.BlockSpec((1, ts, D), lambda b, s: (b, s, 0)),
            pl.BlockSpec((1, ts, wa), lambda b, s: (b, s, 0)),
            pl.BlockSpec((ts, wf), lambda b, s: (s, b)),
            pl.BlockSpec((1, N_MOD, D), lambda b, s: (b, 0, 0)),
            const(fw_bd.shape), const(fb.shape), const(ga.shape), const(gb.shape),
            const(wout.shape), const(g2.shape),
        ],
        out_specs=[
            pl.BlockSpec((1, ts, D), lambda b, s: (b, s, 0)),
            pl.BlockSpec((1, ts, D), lambda b, s: (b, s, 0)),
        ],
        out_shape=[
            jax.ShapeDtypeStruct((B, S, D), F32),
            jax.ShapeDtypeStruct((B, S, D), BF16),
        ],
        compiler_params=_params("parallel", "parallel"),
        name="mixer_out",
    )(x, ya, fr, mod, fw_bd, fb, ga, gb, wout, g2)


N_EXTRACT = PEER_TOPK + 1


def _top_desc(s, n):
    rows = []
    for k in range(n):
        m = jnp.max(s, axis=0, keepdims=True)
        rows.append(m)
        if k + 1 < n:
            s = jnp.where(s == m, NEG_INF, s)
    return rows


def _stack_rows(rows, nrows):
    lanes = rows[0].shape[1]
    idx = lax.broadcasted_iota(jnp.int32, (nrows, lanes), 0)
    out = jnp.full((nrows, lanes), NEG_INF, F32)
    for k, r in enumerate(rows):
        out = jnp.where(idx == k, r, out)
    return out


def _peer_scores_kernel(h2_ref, wq_ref, k1_ref, k2_ref, t1_ref, e1_ref, s2_ref, w2_ref, q_scr):
    tt = h2_ref.shape[0]
    half = k1_ref.shape[2]
    LG = 128
    q = jnp.dot(h2_ref[...], wq_ref[...], preferred_element_type=F32).astype(BF16)
    for h in range(PEER_HEADS):
        q_scr[h] = q[:, h * 2 * half:(h + 1) * 2 * half]
    nt = (((1,), (1,)), ((), ()))
    npad = -(-N_EXTRACT // 8) * 8

    def head(h, carry):
        qh = q_scr[h]
        s1f = lax.dot_general(k1_ref[h], qh[:, :half], nt, preferred_element_type=F32)
        s2f = lax.dot_general(k2_ref[h], qh[:, half:], nt, preferred_element_type=F32)
        s2_ref[h] = s2f
        for li in range(tt // LG):
            ls = slice(li * LG, (li + 1) * LG)
            s1, s2 = s1f[:, ls], s2f[:, ls]
            r1 = _top_desc(s1, N_EXTRACT)
            r2 = _top_desc(s2, N_EXTRACT)
            v1 = _stack_rows(r1, npad)
            v2 = _stack_rows(r2, npad)
            cand = jnp.concatenate(
                [r1[0] + v2] + [r1[j] + v2[:8] for j in range(1, 8)] + [v1[8:] + r2[0]], axis=0)
            top = _top_desc(cand, N_EXTRACT)
            mid = 0.5 * (top[PEER_TOPK - 1] + top[PEER_TOPK])
            m1, m2 = r1[0], r2[0]
            zsum = jnp.sum(jnp.where(cand > mid, jnp.exp(cand - (m1 + m2)), 0.0), axis=0, keepdims=True)
            t1_ref[h, :, ls] = mid - s1
            e1_ref[h, :, ls] = jnp.exp(s1 - m1) / zsum
            w2_ref[h, :, ls] = jnp.exp(s2 - m2)
        return carry

    lax.fori_loop(0, PEER_HEADS, head, 0)


def _peer_scores_call(h2, wq, k1, k2, tt):
    N, D = h2.shape
    H, K, half = k1.shape
    const = lambda shape: pl.BlockSpec(shape, lambda i: (0,) * len(shape))
    out_spec = pl.BlockSpec((H, K, tt), lambda i: (0, 0, i))
    out_shape = jax.ShapeDtypeStruct((H, K, N), F32)
    return pl.pallas_call(
        _peer_scores_kernel,
        grid=(N // tt,),
        in_specs=[pl.BlockSpec((tt, D), lambda i: (i, 0)), const(wq.shape), const(k1.shape), const(k2.shape)],
        out_specs=[out_spec] * 4,
        out_shape=[out_shape] * 4,
        scratch_shapes=[pltpu.VMEM((H, tt, 2 * half), BF16)],
        compiler_params=_params("parallel"),
        name="peer_scores",
    )(h2, wq, k1, k2)


def _peer_dense_kernel(h2_ref, t1_ref, e1_ref, s2_ref, w2_ref, u_ref, vt_ref, x1_ref, mod_ref, fg_ref,
                       o_ref, acc_ref, coef_ref, *, final):
    j = pl.program_id(1)
    tt = h2_ref.shape[0]
    eb = u_ref.shape[0]
    K = PEER_KEYS
    LG = 128
    nt = (((1,), (1,)), ((), ()))

    @pl.when(j == 0)
    def _():
        acc_ref[...] = jnp.zeros_like(acc_ref)

    act = _gelu(lax.dot_general(u_ref[...], h2_ref[...], nt, preferred_element_type=F32))
    rows = eb // K
    r0 = pl.multiple_of(j * rows, rows)
    for li in range(tt // LG):
        ls = slice(li * LG, (li + 1) * LG)
        thr = [t1_ref[h, pl.ds(r0, rows), ls] for h in range(PEER_HEADS)]
        e1 = [e1_ref[h, pl.ds(r0, rows), ls] for h in range(PEER_HEADS)]
        for ri in range(rows):
            g = jnp.zeros((K, LG), F32)
            for h in range(PEER_HEADS):
                sel = jnp.where(s2_ref[h, :, ls] > thr[h][ri:ri + 1, :], w2_ref[h, :, ls], 0.0)
                g = g + sel * e1[h][ri:ri + 1, :]
            coef_ref[ri * K:(ri + 1) * K, ls] = (g * act[ri * K:(ri + 1) * K, ls]).astype(BF16)
    acc_ref[...] += jnp.dot(vt_ref[...], coef_ref[...], preferred_element_type=F32)

    @pl.when(j == pl.num_programs(1) - 1)
    def _():
        x2 = x1_ref[...] + mod_ref[0, 5:6, :] * acc_ref[...].T
        if final:
            x2 = _rms(x2, fg_ref[...])
        o_ref[...] = x2


def _peer_dense_call(h2, t1, e1, s2, w2, u, vt, x1, mod, fg, tt, eb, tokens_per_batch, final):
    N, D = h2.shape
    H, K, _ = t1.shape
    E = u.shape[0]
    tpb = tokens_per_batch // tt
    sc_spec = pl.BlockSpec((H, K, tt), lambda i, j: (0, 0, i))
    return pl.pallas_call(
        functools.partial(_peer_dense_kernel, final=final),
        grid=(N // tt, E // eb),
        in_specs=[
            pl.BlockSpec((tt, D), lambda i, j: (i, 0)),
            sc_spec, sc_spec, sc_spec, sc_spec,
            pl.BlockSpec((eb, D), lambda i, j: (j, 0)),
            pl.BlockSpec((D, eb), lambda i, j: (0, j)),
            pl.BlockSpec((tt, D), lambda i, j: (i, 0)),
            pl.BlockSpec((1, N_MOD, D), lambda i, j: (i // tpb, 0, 0)),
            pl.BlockSpec((1, D), lambda i, j: (0, 0)),
        ],
        out_specs=pl.BlockSpec((tt, D), lambda i, j: (i, 0)),
        out_shape=jax.ShapeDtypeStruct((N, D), F32),
        scratch_shapes=[pltpu.VMEM((D, tt), F32), pltpu.VMEM((eb, tt), BF16)],
        compiler_params=_params("parallel", "arbitrary"),
        name="peer_dense",
    )(h2, t1, e1, s2, w2, u, vt, x1, mod, fg)


def _dft_tables(n):
    k = np.arange(n, dtype=np.int64)
    ang = 2.0 * np.pi * ((k[:, None] * k[None, :]) % n) / n
    return np.cos(ang), np.sin(ang)


def _seq_dft_mats(S, scale):
    k = jnp.arange(S, dtype=jnp.int32)
    ang = ((k[:, None] * k[None, :]) % S).astype(F32) * (2.0 * math.pi / S)
    return (jnp.cos(ang) * scale).astype(BF16), (jnp.sin(ang) * (-scale)).astype(BF16)


def _block_diag(blocks):
    G, a, b = blocks.shape
    eye = jnp.eye(G, dtype=blocks.dtype)
    return (eye[:, None, :, None] * blocks[:, :, None, :]).reshape(G * a, G * b)


def kernel(x, c, mod_w, mod_b, norm1_g, w_in, gmlp_ws, gmlp_bs, fnet_w, fnet_b, gain_a, gain_b, w_out,
           norm2_g, peer_wq, peer_k1, peer_k2, peer_u, peer_v, final_g):
    B, S, D = x.shape
    L = mod_w.shape[0]
    wa = gain_a.shape[1]
    wf = gain_b.shape[1]
    hd = wa // GMLP_HEADS
    gd = wf // FNET_GROUPS
    N = B * S
    ts = min(512, S)
    tt = min(512, S)
    eb = 8 * PEER_KEYS

    mod = _mod_call(c, mod_w, mod_b).reshape(L, B, N_MOD, D)

    avg = _block_diag(jnp.full((GMLP_HEADS, hd, hd), 1.0 / hd, F32)).astype(BF16)
    hmask = _block_diag(jnp.ones((GMLP_HEADS, CHUNK, hd), F32)).astype(BF16)
    cos_d, sin_d = _dft_tables(gd)
    cd = _block_diag(jnp.asarray(np.broadcast_to(cos_d, (FNET_GROUPS, gd, gd)), F32)).astype(BF16)
    sd = _block_diag(jnp.asarray(np.broadcast_to(sin_d, (FNET_GROUPS, gd, gd)), F32)).astype(BF16)
    wc, wsn = _seq_dft_mats(S, 1.0 / math.sqrt(S * gd))

    for l in range(L):
        ws_cat = jnp.transpose(gmlp_ws[l], (1, 0, 2)).reshape(CHUNK, GMLP_HEADS * CHUNK).astype(BF16)
        bs_full = jnp.repeat(gmlp_bs[l].T, hd, axis=1)
        ya, fc, fs = _mixer_in_call(x, mod[l], norm1_g[l][None], w_in[l].astype(BF16), avg, ws_cat, bs_full,
                                    hmask, cd, sd, ts)
        fr = _seq_dft_call(wc, wsn, fc, fs, min(512, S), wf)
        x1, h2 = _mixer_out_call(x, ya, fr, mod[l], _block_diag(fnet_w[l]).astype(BF16),
                                 fnet_b[l].reshape(1, wf), gain_a[l][None], gain_b[l][None],
                                 w_out[l].astype(BF16), norm2_g[l][None], ts)
        h2 = h2.reshape(N, D)
        t1, e1, s2, w2 = _peer_scores_call(h2, peer_wq[l].astype(BF16), peer_k1[l].astype(BF16),
                                           peer_k2[l].astype(BF16), tt)
        x = _peer_dense_call(h2, t1, e1, s2, w2, peer_u[l].astype(BF16), peer_v[l].T.astype(BF16),
                             x1.reshape(N, D), mod[l], final_g[None], tt, eb, S, l == L - 1).reshape(B, S, D)
    return x
```

```python
import functools
import math

import numpy as np
import jax
import jax.numpy as jnp
from jax import lax
from jax.experimental import pallas as pl
from jax.experimental.pallas import tpu as pltpu

GMLP_HEADS = 8
FNET_GROUPS = 8
CHUNK = 128
PEER_HEADS = 8
PEER_KEYS = 128
PEER_TOPK = 16
N_MOD = 6
EPS = 1e-6

VMEM_LIMIT_BYTES = 56 * 1024 * 1024

F32 = jnp.float32
BF16 = jnp.bfloat16
NEG_INF = float("-inf")


def _params(*sem):
    return pltpu.CompilerParams(dimension_semantics=sem, vmem_limit_bytes=VMEM_LIMIT_BYTES)


def _gelu(x):
    return 0.5 * x * (1.0 + lax.erf(x * (1.0 / math.sqrt(2.0))))


def _rms(x, g):
    return x * lax.rsqrt(jnp.mean(x * x, axis=-1, keepdims=True) + EPS) * g


def _mod_kernel(c_ref, w_ref, b_ref, o_ref):
    c = c_ref[...]
    sc = c * jax.nn.sigmoid(c)
    o_ref[0] = jnp.dot(sc, w_ref[0], preferred_element_type=F32) + b_ref[0]


def _mod_call(c, mod_w, mod_b):
    L, D, M = mod_w.shape
    B = c.shape[0]
    tn = 1536
    return pl.pallas_call(
        _mod_kernel,
        grid=(L, M // tn),
        in_specs=[
            pl.BlockSpec((B, D), lambda l, j: (0, 0)),
            pl.BlockSpec((1, D, tn), lambda l, j: (l, 0, j)),
            pl.BlockSpec((1, 1, tn), lambda l, j: (l, 0, j)),
        ],
        out_specs=pl.BlockSpec((1, B, tn), lambda l, j: (l, 0, j)),
        out_shape=jax.ShapeDtypeStruct((L, B, M), F32),
        compiler_params=_params("parallel", "parallel"),
        name="adaln_mod",
    )(c, mod_w, mod_b.reshape(L, 1, M))


def _mixer_in_kernel(x_ref, mod_ref, g_ref, win_ref, avg_ref, ws_ref, bs_ref, hmask_ref, cd_ref, sd_ref,
                     ya_ref, fc_ref, fs_ref):
    ts = x_ref.shape[1]
    wa = ya_ref.shape[2]
    x = x_ref[0]
    h = _rms(x, g_ref[...]) * (1.0 + mod_ref[0, 1:2, :]) + mod_ref[0, 0:1, :]
    z = jnp.dot(h.astype(BF16), win_ref[...], preferred_element_type=F32)
    u = _gelu(z[:, :wa])
    gv = _gelu(z[:, wa:2 * wa])
    zf = z[:, 2 * wa:].astype(BF16)
    mu = jnp.dot(gv.astype(BF16), avg_ref[...], preferred_element_type=F32)
    dv = gv - mu
    var = jnp.dot((dv * dv).astype(BF16), avg_ref[...], preferred_element_type=F32)
    v = (dv * lax.rsqrt(var + EPS)).astype(BF16)
    hmask = hmask_ref[...]
    for ci in range(ts // CHUNK):
        vc = v[ci * CHUNK:(ci + 1) * CHUNK, :]
        vstack = jnp.concatenate([vc] * GMLP_HEADS, axis=0) * hmask
        sv = jnp.dot(ws_ref[...], vstack, preferred_element_type=F32) + bs_ref[...]
        ya_ref[0, ci * CHUNK:(ci + 1) * CHUNK, :] = u[ci * CHUNK:(ci + 1) * CHUNK, :] * sv
    fc_ref[...] = jnp.dot(zf, cd_ref[...], preferred_element_type=F32).astype(BF16)
    fs_ref[...] = jnp.dot(zf, sd_ref[...], preferred_element_type=F32).astype(BF16)


def _mixer_in_call(x, mod, g, win, avg, ws_cat, bs_full, hmask, cd, sd, ts):
    B, S, D = x.shape
    wa = avg.shape[0]
    wf = cd.shape[0]
    const = lambda shape: pl.BlockSpec(shape, lambda b, s: (0,) * len(shape))
    return pl.pallas_call(
        _mixer_in_kernel,
        grid=(B, S // ts),
        in_specs=[
            pl.BlockSpec((1, ts, D), lambda b, s: (b, s, 0)),
            pl.BlockSpec((1, N_MOD, D), lambda b, s: (b, 0, 0)),
            const(g.shape), const(win.shape), const(avg.shape), const(ws_cat.shape),
            const(bs_full.shape), const(hmask.shape), const(cd.shape), const(sd.shape),
        ],
        out_specs=[
            pl.BlockSpec((1, ts, wa), lambda b, s: (b, s, 0)),
            pl.BlockSpec((ts, wf), lambda b, s: (s, b)),
            pl.BlockSpec((ts, wf), lambda b, s: (s, b)),
        ],
        out_shape=[
            jax.ShapeDtypeStruct((B, S, wa), F32),
            jax.ShapeDtypeStruct((S, B * wf), BF16),
            jax.ShapeDtypeStruct((S, B * wf), BF16),
        ],
        compiler_params=_params("parallel", "parallel"),
        name="mixer_in",
    )(x, mod, g, win, avg, ws_cat, bs_full, hmask, cd, sd)


def _seq_dft_kernel(wc_ref, wsn_ref, fc_ref, fs_ref, o_ref):
    acc = jnp.dot(wc_ref[...], fc_ref[...], preferred_element_type=F32)
    acc += jnp.dot(wsn_ref[...], fs_ref[...], preferred_element_type=F32)
    o_ref[...] = acc.astype(o_ref.dtype)


def _seq_dft_call(wc, wsn, fc, fs, tm, tn):
    S = wc.shape[0]
    ncol = fc.shape[1]
    return pl.pallas_call(
        _seq_dft_kernel,
        grid=(S // tm, ncol // tn),
        in_specs=[
            pl.BlockSpec((tm, S), lambda i, j: (i, 0)),
            pl.BlockSpec((tm, S), lambda i, j: (i, 0)),
            pl.BlockSpec((S, tn), lambda i, j: (0, j)),
            pl.BlockSpec((S, tn), lambda i, j: (0, j)),
        ],
        out_specs=pl.BlockSpec((tm, tn), lambda i, j: (i, j)),
        out_shape=jax.ShapeDtypeStruct((S, ncol), BF16),
        compiler_params=_params("parallel", "parallel"),
        name="seq_dft",
    )(wc, wsn, fc, fs)


def _mixer_out_kernel(x_ref, ya_ref, fr_ref, mod_ref, fw_ref, fb_ref, ga_ref, gb_ref, wout_ref, g2_ref,
                      x1_ref, h2_ref, h2t_ref):
    wa = ya_ref.shape[2]
    yb = jnp.dot(fr_ref[...], fw_ref[...], preferred_element_type=F32) + fb_ref[...]
    na = _rms(ya_ref[0], ga_ref[...]).astype(BF16)
    nb = _rms(yb, gb_ref[...]).astype(BF16)
    mix = jnp.dot(na, wout_ref[:wa, :], preferred_element_type=F32)
    mix += jnp.dot(nb, wout_ref[wa:, :], preferred_element_type=F32)
    x1 = x_ref[0] + mod_ref[0, 2:3, :] * mix
    x1_ref[0] = x1
    h2 = _rms(x1, g2_ref[...]) * (1.0 + mod_ref[0, 4:5, :]) + mod_ref[0, 3:4, :]
    h2_ref[0] = h2.astype(BF16)
    h2t_ref[...] = h2.T.astype(BF16)


def _mixer_out_call(x, ya, fr, mod, fw_bd, fb, ga, gb, wout, g2, ts):
    B, S, D = x.shape
    wa = ya.shape[2]
    wf = fw_bd.shape[0]
    const = lambda shape: pl.BlockSpec(shape, lambda b, s: (0,) * len(shape))
    return pl.pallas_call(
        _mixer_out_kernel,
        grid=(B, S // ts),
        in_specs=[
            pl.BlockSpec((1, ts, D), lambda b, s: (b, s, 0)),
            pl.BlockSpec((1, ts, wa), lambda b, s: (b, s, 0)),
            pl.BlockSpec((ts, wf), lambda b, s: (s, b)),
            pl.BlockSpec((1, N_MOD, D), lambda b, s: (b, 0, 0)),
            const(fw_bd.shape), const(fb.shape), const(ga.shape), const(gb.shape),
            const(wout.shape), const(g2.shape),
        ],
        out_specs=[
            pl.BlockSpec((1, ts, D), lambda b, s: (b, s, 0)),
            pl.BlockSpec((1, ts, D), lambda b, s: (b, s, 0)),
            pl.BlockSpec((D, ts), lambda b, s: (0, b * (S // ts) + s)),
        ],
        out_shape=[
            jax.ShapeDtypeStruct((B, S, D), F32),
            jax.ShapeDtypeStruct((B, S, D), BF16),
            jax.ShapeDtypeStruct((D, B * S), BF16),
        ],
        compiler_params=_params("parallel", "parallel"),
        name="mixer_out",
    )(x, ya, fr, mod, fw_bd, fb, ga, gb, wout, g2)


N_EXTRACT = PEER_TOPK + 1


def _top_desc(s, n):
    rows = []
    for k in range(n):
        m = jnp.max(s, axis=0, keepdims=True)
        rows.append(m)
        if k + 1 < n:
            s = jnp.where(s == m, NEG_INF, s)
    return rows


def _stack_rows(rows, nrows):
    lanes = rows[0].shape[1]
    idx = lax.broadcasted_iota(jnp.int32, (nrows, lanes), 0)
    out = jnp.full((nrows, lanes), NEG_INF, F32)
    for k, r in enumerate(rows):
        out = jnp.where(idx == k, r, out)
    return out


def _peer_scores_kernel(h2_ref, wq_ref, k1_ref, k2_ref, t1_ref, e1_ref, s2_ref, w2_ref, q_scr):
    tt = h2_ref.shape[0]
    half = k1_ref.shape[2]
    LG = 128
    q = jnp.dot(h2_ref[...], wq_ref[...], preferred_element_type=F32).astype(BF16)
    for h in range(PEER_HEADS):
        q_scr[h] = q[:, h * 2 * half:(h + 1) * 2 * half]
    nt = (((1,), (1,)), ((), ()))
    npad = -(-N_EXTRACT // 8) * 8

    def head(h, carry):
        qh = q_scr[h]
        s1f = lax.dot_general(k1_ref[h], qh[:, :half], nt, preferred_element_type=F32)
        s2f = lax.dot_general(k2_ref[h], qh[:, half:], nt, preferred_element_type=F32)
        for li in range(tt // LG):
            ls = slice(li * LG, (li + 1) * LG)
            s1, s2 = s1f[:, ls], s2f[:, ls]
            s2_ref[h, li] = s2
            r1 = _top_desc(s1, N_EXTRACT)
            r2 = _top_desc(s2, N_EXTRACT)
            v1 = _stack_rows(r1, npad)
            v2 = _stack_rows(r2, npad)
            cand = jnp.concatenate(
                [r1[0] + v2] + [r1[j] + v2[:8] for j in range(1, 8)] + [v1[8:] + r2[0]], axis=0)
            top = _top_desc(cand, N_EXTRACT)
            mid = 0.5 * (top[PEER_TOPK - 1] + top[PEER_TOPK])
            m1, m2 = r1[0], r2[0]
            zsum = jnp.sum(jnp.where(cand > mid, jnp.exp(cand - (m1 + m2)), 0.0), axis=0, keepdims=True)
            t1_ref[h, :, ls] = mid - s1
            e1_ref[h, :, ls] = jnp.exp(s1 - m1) * (0.5 / zsum)
            w2_ref[h, li] = jnp.exp(s2 - m2)
        return carry

    lax.fori_loop(0, PEER_HEADS, head, 0)


def _peer_scores_call(h2, wq, k1, k2, tt):
    N, D = h2.shape
    H, K, half = k1.shape
    const = lambda shape: pl.BlockSpec(shape, lambda i: (0,) * len(shape))
    LG = 128
    row_spec = pl.BlockSpec((H, K, tt), lambda i: (0, 0, i))
    row_shape = jax.ShapeDtypeStruct((H, K, N), F32)
    key_spec = pl.BlockSpec((H, tt // LG, K, LG), lambda i: (0, i, 0, 0))
    key_shape = jax.ShapeDtypeStruct((H, N // LG, K, LG), F32)
    return pl.pallas_call(
        _peer_scores_kernel,
        grid=(N // tt,),
        in_specs=[pl.BlockSpec((tt, D), lambda i: (i, 0)), const(wq.shape), const(k1.shape), const(k2.shape)],
        out_specs=[row_spec, row_spec, key_spec, key_spec],
        out_shape=[row_shape, row_shape, key_shape, key_shape],
        scratch_shapes=[pltpu.VMEM((H, tt, 2 * half), BF16)],
        compiler_params=_params("parallel"),
        name="peer_scores",
    )(h2, wq, k1, k2)


def _peer_dense_kernel(h2t_ref, t1_ref, e1_ref, s2_ref, w2_ref, u_ref, vt_ref, x1_ref, mod_ref, fg_ref,
                       o_ref, acc_ref, p_ref, c_ref, *, final):
    j = pl.program_id(1)
    tt = h2t_ref.shape[1]
    eb = u_ref.shape[0]
    K = PEER_KEYS
    LG = 128
    QR = 32
    nrow = eb // K

    @pl.when(j == 0)
    def _():
        acc_ref[...] = jnp.zeros_like(acc_ref)

    pre = jnp.dot(u_ref[...], h2t_ref[...], preferred_element_type=F32)
    for li in range(tt // LG):
        p_ref[li] = pre[:, li * LG:(li + 1) * LG]
    for li in range(tt // LG):
        ls = slice(li * LG, (li + 1) * LG)
        for q in range(K // QR):
            qs = slice(q * QR, (q + 1) * QR)
            gate = [None] * nrow
            for h in range(PEER_HEADS):
                s2 = s2_ref[h, li, qs, :]
                w2 = w2_ref[h, li, qs, :]
                for ri in range(nrow):
                    term = jnp.where(s2 > t1_ref[ri, h:h + 1, ls], w2, 0.0) * e1_ref[ri, h:h + 1, ls]
                    gate[ri] = term if gate[ri] is None else gate[ri] + term
            for ri in range(nrow):
                rows = slice(ri * K + q * QR, ri * K + (q + 1) * QR)
                p = p_ref[li, rows, :]
                coef = gate[ri] * (p * (1.0 + lax.erf(p * (1.0 / math.sqrt(2.0)))))
                c_ref[rows, ls] = coef.astype(BF16)
    acc_ref[...] += jnp.dot(vt_ref[...], c_ref[...], preferred_element_type=F32)

    @pl.when(j == pl.num_programs(1) - 1)
    def _():
        x2 = x1_ref[...] + mod_ref[0, 5:6, :] * acc_ref[...].T
        if final:
            x2 = _rms(x2, fg_ref[...])
        o_ref[...] = x2


def _peer_dense_call(h2t, t1, e1, s2, w2, u, vt, x1, mod, fg, tt, eb, tokens_per_batch, final):
    D, N = h2t.shape
    H, _, K, LG = s2.shape
    E = u.shape[0]
    tpb = tokens_per_batch // tt
    rows = eb // K
    row_spec = pl.BlockSpec((rows, H, tt), lambda i, j: (j, 0, i))
    key_spec = pl.BlockSpec((H, tt // LG, K, LG), lambda i, j: (0, i, 0, 0))
    return pl.pallas_call(
        functools.partial(_peer_dense_kernel, final=final),
        grid=(N // tt, E // eb),
        in_specs=[
            pl.BlockSpec((D, tt), lambda i, j: (0, i)),
            row_spec, row_spec, key_spec, key_spec,
            pl.BlockSpec((eb, D), lambda i, j: (j, 0)),
            pl.BlockSpec((D, eb), lambda i, j: (0, j)),
            pl.BlockSpec((tt, D), lambda i, j: (i, 0)),
            pl.BlockSpec((1, N_MOD, D), lambda i, j: (i // tpb, 0, 0)),
            pl.BlockSpec((1, D), lambda i, j: (0, 0)),
        ],
        out_specs=pl.BlockSpec((tt, D), lambda i, j: (i, 0)),
        out_shape=jax.ShapeDtypeStruct((N, D), F32),
        scratch_shapes=[pltpu.VMEM((D, tt), F32), pltpu.VMEM((tt // LG, eb, LG), F32),
                        pltpu.VMEM((eb, tt), BF16)],
        compiler_params=_params("parallel", "arbitrary"),
        name="peer_dense",
    )(h2t, t1, e1, s2, w2, u, vt, x1, mod, fg)


def _dft_tables(n):
    k = np.arange(n, dtype=np.int64)
    ang = 2.0 * np.pi * ((k[:, None] * k[None, :]) % n) / n
    return np.cos(ang), np.sin(ang)


def _seq_dft_mats(S, scale):
    k = jnp.arange(S, dtype=jnp.int32)
    ang = ((k[:, None] * k[None, :]) % S).astype(F32) * (2.0 * math.pi / S)
    return (jnp.cos(ang) * scale).astype(BF16), (jnp.sin(ang) * (-scale)).astype(BF16)


def _block_diag(blocks):
    G, a, b = blocks.shape
    eye = jnp.eye(G, dtype=blocks.dtype)
    return (eye[:, None, :, None] * blocks[:, :, None, :]).reshape(G * a, G * b)


def kernel(x, c, mod_w, mod_b, norm1_g, w_in, gmlp_ws, gmlp_bs, fnet_w, fnet_b, gain_a, gain_b, w_out,
           norm2_g, peer_wq, peer_k1, peer_k2, peer_u, peer_v, final_g):
    B, S, D = x.shape
    L = mod_w.shape[0]
    wa = gain_a.shape[1]
    wf = gain_b.shape[1]
    hd = wa // GMLP_HEADS
    gd = wf // FNET_GROUPS
    N = B * S
    ts = min(512, S)
    tt = min(512, S)
    eb = 8 * PEER_KEYS

    mod = _mod_call(c, mod_w, mod_b).reshape(L, B, N_MOD, D)

    avg = _block_diag(jnp.full((GMLP_HEADS, hd, hd), 1.0 / hd, F32)).astype(BF16)
    hmask = _block_diag(jnp.ones((GMLP_HEADS, CHUNK, hd), F32)).astype(BF16)
    cos_d, sin_d = _dft_tables(gd)
    cd = _block_diag(jnp.asarray(np.broadcast_to(cos_d, (FNET_GROUPS, gd, gd)), F32)).astype(BF16)
    sd = _block_diag(jnp.asarray(np.broadcast_to(sin_d, (FNET_GROUPS, gd, gd)), F32)).astype(BF16)
    wc, wsn = _seq_dft_mats(S, 1.0 / math.sqrt(S * gd))

    for l in range(L):
        ws_cat = jnp.transpose(gmlp_ws[l], (1, 0, 2)).reshape(CHUNK, GMLP_HEADS * CHUNK).astype(BF16)
        bs_full = jnp.repeat(gmlp_bs[l].T, hd, axis=1)
        ya, fc, fs = _mixer_in_call(x, mod[l], norm1_g[l][None], w_in[l].astype(BF16), avg, ws_cat, bs_full,
                                    hmask, cd, sd, ts)
        fr = _seq_dft_call(wc, wsn, fc, fs, min(512, S), wf)
        x1, h2, h2t = _mixer_out_call(x, ya, fr, mod[l], _block_diag(fnet_w[l]).astype(BF16),
                                 fnet_b[l].reshape(1, wf), gain_a[l][None], gain_b[l][None],
                                 w_out[l].astype(BF16), norm2_g[l][None], ts)
        h2 = h2.reshape(N, D)
        t1, e1, s2, w2 = _peer_scores_call(h2, peer_wq[l].astype(BF16), peer_k1[l].astype(BF16),
                                           peer_k2[l].astype(BF16), tt)
        t1, e1 = jnp.transpose(t1, (1, 0, 2)), jnp.transpose(e1, (1, 0, 2))
        x = _peer_dense_call(h2t, t1, e1, s2, w2, peer_u[l].astype(BF16), peer_v[l].T.astype(BF16),
                             x1.reshape(N, D), mod[l], final_g[None], tt, eb, S, l == L - 1).reshape(B, S, D)
    return x
```

```python
import functools
import math

import numpy as np
import jax
import jax.numpy as jnp
from jax import lax
from jax.experimental import pallas as pl
from jax.experimental.pallas import tpu as pltpu

GMLP_HEADS = 8
FNET_GROUPS = 8
CHUNK = 128
PEER_HEADS = 8
PEER_KEYS = 128
PEER_TOPK = 16
N_MOD = 6
EPS = 1e-6

VMEM_LIMIT_BYTES = 56 * 1024 * 1024

F32 = jnp.float32
BF16 = jnp.bfloat16
NEG_INF = float("-inf")


def _params(*sem):
    return pltpu.CompilerParams(dimension_semantics=sem, vmem_limit_bytes=VMEM_LIMIT_BYTES)


def _gelu(x):
    return 0.5 * x * (1.0 + lax.erf(x * (1.0 / math.sqrt(2.0))))


def _rms(x, g):
    return x * lax.rsqrt(jnp.mean(x * x, axis=-1, keepdims=True) + EPS) * g


def _mod_kernel(c_ref, w_ref, b_ref, o_ref):
    c = c_ref[...]
    sc = c * jax.nn.sigmoid(c)
    o_ref[0] = jnp.dot(sc, w_ref[0], preferred_element_type=F32) + b_ref[0]


def _mod_call(c, mod_w, mod_b):
    L, D, M = mod_w.shape
    B = c.shape[0]
    tn = 1536
    return pl.pallas_call(
        _mod_kernel,
        grid=(L, M // tn),
        in_specs=[
            pl.BlockSpec((B, D), lambda l, j: (0, 0)),
            pl.BlockSpec((1, D, tn), lambda l, j: (l, 0, j)),
            pl.BlockSpec((1, 1, tn), lambda l, j: (l, 0, j)),
        ],
        out_specs=pl.BlockSpec((1, B, tn), lambda l, j: (l, 0, j)),
        out_shape=jax.ShapeDtypeStruct((L, B, M), F32),
        compiler_params=_params("parallel", "parallel"),
        name="adaln_mod",
    )(c, mod_w, mod_b.reshape(L, 1, M))


def _mixer_in_kernel(x_ref, mod_ref, g_ref, win_ref, avg_ref, ws_ref, bs_ref, hmask_ref, cd_ref, sd_ref,
                     ya_ref, fc_ref, fs_ref):
    ts = x_ref.shape[1]
    wa = ya_ref.shape[2]
    x = x_ref[0]
    h = _rms(x, g_ref[...]) * (1.0 + mod_ref[0, 1:2, :]) + mod_ref[0, 0:1, :]
    z = jnp.dot(h.astype(BF16), win_ref[...], preferred_element_type=F32)
    u = _gelu(z[:, :wa])
    gv = _gelu(z[:, wa:2 * wa])
    zf = z[:, 2 * wa:].astype(BF16)
    mu = jnp.dot(gv.astype(BF16), avg_ref[...], preferred_element_type=F32)
    dv = gv - mu
    var = jnp.dot((dv * dv).astype(BF16), avg_ref[...], preferred_element_type=F32)
    v = (dv * lax.rsqrt(var + EPS)).astype(BF16)
    hmask = hmask_ref[...]
    for ci in range(ts // CHUNK):
        vc = v[ci * CHUNK:(ci + 1) * CHUNK, :]
        vstack = jnp.concatenate([vc] * GMLP_HEADS, axis=0) * hmask
        sv = jnp.dot(ws_ref[...], vstack, preferred_element_type=F32) + bs_ref[...]
        ya_ref[0, ci * CHUNK:(ci + 1) * CHUNK, :] = u[ci * CHUNK:(ci + 1) * CHUNK, :] * sv
    fc_ref[...] = jnp.dot(zf, cd_ref[...], preferred_element_type=F32).astype(BF16)
    fs_ref[...] = jnp.dot(zf, sd_ref[...], preferred_element_type=F32).astype(BF16)


def _mixer_in_call(x, mod, g, win, avg, ws_cat, bs_full, hmask, cd, sd, ts):
    B, S, D = x.shape
    wa = avg.shape[0]
    wf = cd.shape[0]
    const = lambda shape: pl.BlockSpec(shape, lambda b, s: (0,) * len(shape))
    return pl.pallas_call(
        _mixer_in_kernel,
        grid=(B, S // ts),
        in_specs=[
            pl.BlockSpec((1, ts, D), lambda b, s: (b, s, 0)),
            pl.BlockSpec((1, N_MOD, D), lambda b, s: (b, 0, 0)),
            const(g.shape), const(win.shape), const(avg.shape), const(ws_cat.shape),
            const(bs_full.shape), const(hmask.shape), const(cd.shape), const(sd.shape),
        ],
        out_specs=[
            pl.BlockSpec((1, ts, wa), lambda b, s: (b, s, 0)),
            pl.BlockSpec((ts, wf), lambda b, s: (s, b)),
            pl.BlockSpec((ts, wf), lambda b, s: (s, b)),
        ],
        out_shape=[
            jax.ShapeDtypeStruct((B, S, wa), F32),
            jax.ShapeDtypeStruct((S, B * wf), BF16),
            jax.ShapeDtypeStruct((S, B * wf), BF16),
        ],
        compiler_params=_params("parallel", "parallel"),
        name="mixer_in",
    )(x, mod, g, win, avg, ws_cat, bs_full, hmask, cd, sd)


def _seq_dft_kernel(wc_ref, wsn_ref, fc_ref, fs_ref, o_ref):
    acc = jnp.dot(wc_ref[...], fc_ref[...], preferred_element_type=F32)
    acc += jnp.dot(wsn_ref[...], fs_ref[...], preferred_element_type=F32)
    o_ref[...] = acc.astype(o_ref.dtype)


def _seq_dft_call(wc, wsn, fc, fs, tm, tn):
    S = wc.shape[0]
    ncol = fc.shape[1]
    return pl.pallas_call(
        _seq_dft_kernel,
        grid=(S // tm, ncol // tn),
        in_specs=[
            pl.BlockSpec((tm, S), lambda i, j: (i, 0)),
            pl.BlockSpec((tm, S), lambda i, j: (i, 0)),
            pl.BlockSpec((S, tn), lambda i, j: (0, j)),
            pl.BlockSpec((S, tn), lambda i, j: (0, j)),
        ],
        out_specs=pl.BlockSpec((tm, tn), lambda i, j: (i, j)),
        out_shape=jax.ShapeDtypeStruct((S, ncol), BF16),
        compiler_params=_params("parallel", "parallel"),
        name="seq_dft",
    )(wc, wsn, fc, fs)


def _mixer_out_kernel(x_ref, ya_ref, fr_ref, mod_ref, fw_ref, fb_ref, ga_ref, gb_ref, wout_ref, g2_ref,
                      x1_ref, h2_ref, h2t_ref):
    wa = ya_ref.shape[2]
    yb = jnp.dot(fr_ref[...], fw_ref[...], preferred_element_type=F32) + fb_ref[...]
    na = _rms(ya_ref[0], ga_ref[...]).astype(BF16)
    nb = _rms(yb, gb_ref[...]).astype(BF16)
    mix = jnp.dot(na, wout_ref[:wa, :], preferred_element_type=F32)
    mix += jnp.dot(nb, wout_ref[wa:, :], preferred_element_type=F32)
    x1 = x_ref[0] + mod_ref[0, 2:3, :] * mix
    x1_ref[0] = x1
    h2 = _rms(x1, g2_ref[...]) * (1.0 + mod_ref[0, 4:5, :]) + mod_ref[0, 3:4, :]
    h2_ref[0] = h2.astype(BF16)
    h2t_ref[...] = h2.T.astype(BF16)


def _mixer_out_call(x, ya, fr, mod, fw_bd, fb, ga, gb, wout, g2, ts):
    B, S, D = x.shape
    wa = ya.shape[2]
    wf = fw_bd.shape[0]
    const = lambda shape: pl.BlockSpec(shape, lambda b, s: (0,) * len(shape))
    return pl.pallas_call(
        _mixer_out_kernel,
        grid=(B, S // ts),
        in_specs=[
            pl.BlockSpec((1, ts, D), lambda b, s: (b, s, 0)),
            pl.BlockSpec((1, ts, wa), lambda b, s: (b, s, 0)),
            pl.BlockSpec((ts, wf), lambda b, s: (s, b)),
            pl.BlockSpec((1, N_MOD, D), lambda b, s: (b, 0, 0)),
            const(fw_bd.shape), const(fb.shape), const(ga.shape), const(gb.shape),
            const(wout.shape), const(g2.shape),
        ],
        out_specs=[
            pl.BlockSpec((1, ts, D), lambda b, s: (b, s, 0)),
            pl.BlockSpec((1, ts, D), lambda b, s: (b, s, 0)),
            pl.BlockSpec((D, ts), lambda b, s: (0, b * (S // ts) + s)),
        ],
        out_shape=[
            jax.ShapeDtypeStruct((B, S, D), F32),
            jax.ShapeDtypeStruct((B, S, D), BF16),
            jax.ShapeDtypeStruct((D, B * S), BF16),
        ],
        compiler_params=_params("parallel", "parallel"),
        name="mixer_out",
    )(x, ya, fr, mod, fw_bd, fb, ga, gb, wout, g2)


N_EXTRACT = PEER_TOPK + 1


def _top_desc(s, n):
    rows = []
    for k in range(n):
        m = jnp.max(s, axis=0, keepdims=True)
        rows.append(m)
        if k + 1 < n:
            s = jnp.where(s == m, NEG_INF, s)
    return rows


def _stack_rows(rows, nrows):
    lanes = rows[0].shape[1]
    idx = lax.broadcasted_iota(jnp.int32, (nrows, lanes), 0)
    out = jnp.full((nrows, lanes), NEG_INF, F32)
    for k, r in enumerate(rows):
        out = jnp.where(idx == k, r, out)
    return out


def _peer_scores_kernel(h2_ref, wq_ref, k1_ref, k2_ref, t1_ref, e1_ref, s2_ref, w2_ref, q_scr):
    tt = h2_ref.shape[0]
    half = k1_ref.shape[2]
    LG = 128
    q = jnp.dot(h2_ref[...], wq_ref[...], preferred_element_type=F32).astype(BF16)
    for h in range(PEER_HEADS):
        q_scr[h] = q[:, h * 2 * half:(h + 1) * 2 * half]
    nt = (((1,), (1,)), ((), ()))
    npad = -(-N_EXTRACT // 8) * 8

    def head(h, carry):
        qh = q_scr[h]
        s1f = lax.dot_general(k1_ref[h], qh[:, :half], nt, preferred_element_type=F32)
        s2f = lax.dot_general(k2_ref[h], qh[:, half:], nt, preferred_element_type=F32)
        for li in range(tt // LG):
            ls = slice(li * LG, (li + 1) * LG)
            s1, s2 = s1f[:, ls], s2f[:, ls]
            s2_ref[h, li] = s2
            r1 = _top_desc(s1, N_EXTRACT)
            r2 = _top_desc(s2, N_EXTRACT)
            v1 = _stack_rows(r1, npad)
            v2 = _stack_rows(r2, npad)
            cand = jnp.concatenate(
                [r1[0] + v2] + [r1[j] + v2[:8] for j in range(1, 8)] + [v1[8:] + r2[0]], axis=0)
            top = _top_desc(cand, N_EXTRACT)
            mid = 0.5 * (top[PEER_TOPK - 1] + top[PEER_TOPK])
            m1, m2 = r1[0], r2[0]
            zsum = jnp.sum(jnp.where(cand > mid, jnp.exp(cand - (m1 + m2)), 0.0), axis=0, keepdims=True)
            t1_ref[h, :, ls] = mid - s1
            e1_ref[h, :, ls] = jnp.exp(s1 - m1) * (0.5 / zsum)
            w2_ref[h, li] = jnp.exp(s2 - m2)
        return carry

    lax.fori_loop(0, PEER_HEADS, head, 0)


def _peer_scores_call(h2, wq, k1, k2, tt):
    N, D = h2.shape
    H, K, half = k1.shape
    const = lambda shape: pl.BlockSpec(shape, lambda i: (0,) * len(shape))
    LG = 128
    row_spec = pl.BlockSpec((H, K, tt), lambda i: (0, 0, i))
    row_shape = jax.ShapeDtypeStruct((H, K, N), F32)
    key_spec = pl.BlockSpec((H, tt // LG, K, LG), lambda i: (0, i, 0, 0))
    key_shape = jax.ShapeDtypeStruct((H, N // LG, K, LG), F32)
    return pl.pallas_call(
        _peer_scores_kernel,
        grid=(N // tt,),
        in_specs=[pl.BlockSpec((tt, D), lambda i: (i, 0)), const(wq.shape), const(k1.shape), const(k2.shape)],
        out_specs=[row_spec, row_spec, key_spec, key_spec],
        out_shape=[row_shape, row_shape, key_shape, key_shape],
        scratch_shapes=[pltpu.VMEM((H, tt, 2 * half), BF16)],
        compiler_params=_params("parallel"),
        name="peer_scores",
    )(h2, wq, k1, k2)


def _peer_dense_kernel(h2t_ref, t1_ref, e1_ref, s2_ref, w2_ref, u_ref, vt_ref, x1_ref, mod_ref, fg_ref,
                       o_ref, acc_ref, p0_ref, p1_ref, c0_ref, c1_ref,
                       us_ref, vts_ref, t1s_ref, e1s_ref, s2s_ref, w2s_ref, h2ts_ref, *, final, nb):
    g = pl.program_id(0)
    tt = h2t_ref.shape[1]
    eb = u_ref.shape[0]
    K = PEER_KEYS
    LG = 128

    @pl.when(g == 0)
    def _():
        acc_ref[...] = jnp.zeros_like(acc_ref)
        p1_ref[...] = jnp.zeros_like(p1_ref)
        c0_ref[...] = jnp.zeros_like(c0_ref)

    us_ref[...] = u_ref[...]
    vts_ref[...] = vt_ref[...]
    t1s_ref[...] = t1_ref[...]
    e1s_ref[...] = e1_ref[...]

    @pl.when(g % nb == 0)
    def _():
        h2ts_ref[...] = h2t_ref[...]

    @pl.when(jnp.logical_or(g == 0, (g - 1) % nb == 0))
    def _():
        s2s_ref[...] = s2_ref[...]
        w2s_ref[...] = w2_ref[...]

    def stages(p_new, p_old, c_new, c_old):
        nrow = eb // K
        dc = acc_ref.shape[0] // nrow
        QR = 32
        nq = K // QR
        units = [(li, q) for li in range(tt // LG) for q in range(nq)]
        assert len(units) == 2 * nrow
        for ui, (li, q) in enumerate(units):
            piece = ui // 2
            if ui % 2 == 0:
                rs = slice(piece * K, (piece + 1) * K)
                pre = jnp.dot(us_ref[rs, :], h2ts_ref[...], preferred_element_type=F32)
                for lj in range(tt // LG):
                    p_new[lj, rs, :] = pre[:, lj * LG:(lj + 1) * LG]
            else:
                ds = slice(piece * dc, (piece + 1) * dc)
                acc_ref[ds, :] += jnp.dot(vts_ref[ds, :], c_old[...], preferred_element_type=F32)
            ls = slice(li * LG, (li + 1) * LG)
            qs = slice(q * QR, (q + 1) * QR)
            gate = [None] * nrow
            for h in range(PEER_HEADS):
                s2 = s2s_ref[h, li, qs, :]
                w2 = w2s_ref[h, li, qs, :]
                for ri in range(nrow):
                    term = jnp.where(s2 > t1s_ref[h, ri:ri + 1, ls], w2, 0.0) * e1s_ref[h, ri:ri + 1, ls]
                    gate[ri] = term if gate[ri] is None else gate[ri] + term
            for ri in range(nrow):
                p = p_old[li, ri * K + q * QR:ri * K + (q + 1) * QR, :]
                coef = gate[ri] * (p * (1.0 + lax.erf(p * (1.0 / math.sqrt(2.0)))))
                c_new[ri * K + q * QR:ri * K + (q + 1) * QR, ls] = coef.astype(BF16)

    @pl.when(g % 2 == 0)
    def _():
        stages(p0_ref, p1_ref, c1_ref, c0_ref)

    @pl.when(g % 2 == 1)
    def _():
        stages(p1_ref, p0_ref, c0_ref, c1_ref)

    @pl.when(jnp.logical_and(g >= 2, (g - 2) % nb == nb - 1))
    def _():
        x2 = x1_ref[...] + mod_ref[0, 5:6, :] * acc_ref[...].T
        if final:
            x2 = _rms(x2, fg_ref[...])
        o_ref[...] = x2
        acc_ref[...] = jnp.zeros_like(acc_ref)


def _peer_dense_call(h2t, t1, e1, s2, w2, u, vt, x1, mod, fg, tt, eb, tokens_per_batch, final):
    D, N = h2t.shape
    H, _, K, LG = s2.shape
    E = u.shape[0]
    nb = E // eb
    total = (N // tt) * nb
    tpb = tokens_per_batch // tt
    rows = eb // K
    blk = lambda g, lag: jnp.clip(g - lag, 0, total - 1)
    row_spec = pl.BlockSpec((H, rows, tt), lambda g: (0, blk(g, 1) % nb, blk(g, 1) // nb))
    key_spec = pl.BlockSpec((H, tt // LG, K, LG), lambda g: (0, blk(g, 1) // nb, 0, 0))
    return pl.pallas_call(
        functools.partial(_peer_dense_kernel, final=final, nb=nb),
        grid=(total + 2,),
        in_specs=[
            pl.BlockSpec((D, tt), lambda g: (0, blk(g, 0) // nb)),
            row_spec, row_spec, key_spec, key_spec,
            pl.BlockSpec((eb, D), lambda g: (blk(g, 0) % nb, 0)),
            pl.BlockSpec((D, eb), lambda g: (0, blk(g, 2) % nb)),
            pl.BlockSpec((tt, D), lambda g: (blk(g, 2) // nb, 0)),
            pl.BlockSpec((1, N_MOD, D), lambda g: (blk(g, 2) // nb // tpb, 0, 0)),
            pl.BlockSpec((1, D), lambda g: (0, 0)),
        ],
        out_specs=pl.BlockSpec((tt, D), lambda g: (blk(g, 2) // nb, 0)),
        out_shape=jax.ShapeDtypeStruct((N, D), F32),
        scratch_shapes=[pltpu.VMEM((D, tt), F32),
                        pltpu.VMEM((tt // LG, eb, LG), F32), pltpu.VMEM((tt // LG, eb, LG), F32),
                        pltpu.VMEM((eb, tt), BF16), pltpu.VMEM((eb, tt), BF16),
                        pltpu.VMEM((eb, D), BF16), pltpu.VMEM((D, eb), BF16),
                        pltpu.VMEM((H, rows, tt), F32), pltpu.VMEM((H, rows, tt), F32),
                        pltpu.VMEM((H, tt // LG, K, LG), F32), pltpu.VMEM((H, tt // LG, K, LG), F32),
                        pltpu.VMEM((D, tt), BF16)],
        compiler_params=_params("arbitrary"),
        name="peer_dense",
    )(h2t, t1, e1, s2, w2, u, vt, x1, mod, fg)


def _dft_tables(n):
    k = np.arange(n, dtype=np.int64)
    ang = 2.0 * np.pi * ((k[:, None] * k[None, :]) % n) / n
    return np.cos(ang), np.sin(ang)


def _seq_dft_mats(S, scale):
    R = 1 << (int(math.log2(S)) // 2)
    k = jnp.arange(S, dtype=jnp.int32)[:, None]
    w = 2.0 * math.pi / S
    ang_a = ((k * (jnp.arange(S // R, dtype=jnp.int32)[None, :] * R)) % S).astype(F32) * w
    ang_b = ((k * jnp.arange(R, dtype=jnp.int32)[None, :]) % S).astype(F32) * w
    ca, sa = jnp.cos(ang_a)[:, :, None], jnp.sin(ang_a)[:, :, None]
    cb, sb = jnp.cos(ang_b)[:, None, :], jnp.sin(ang_b)[:, None, :]
    cos_w = (ca * cb - sa * sb).reshape(S, S)
    sin_w = (sa * cb + ca * sb).reshape(S, S)
    return (cos_w * scale).astype(BF16), (sin_w * (-scale)).astype(BF16)


def _block_diag(blocks):
    G, a, b = blocks.shape
    eye = jnp.eye(G, dtype=blocks.dtype)
    return (eye[:, None, :, None] * blocks[:, :, None, :]).reshape(G * a, G * b)


def kernel(x, c, mod_w, mod_b, norm1_g, w_in, gmlp_ws, gmlp_bs, fnet_w, fnet_b, gain_a, gain_b, w_out,
           norm2_g, peer_wq, peer_k1, peer_k2, peer_u, peer_v, final_g):
    B, S, D = x.shape
    L = mod_w.shape[0]
    wa = gain_a.shape[1]
    wf = gain_b.shape[1]
    hd = wa // GMLP_HEADS
    gd = wf // FNET_GROUPS
    N = B * S
    ts = min(512, S)
    tt = min(512, S)
    eb = 8 * PEER_KEYS

    mod = _mod_call(c, mod_w, mod_b).reshape(L, B, N_MOD, D)

    avg = _block_diag(jnp.full((GMLP_HEADS, hd, hd), 1.0 / hd, F32)).astype(BF16)
    hmask = _block_diag(jnp.ones((GMLP_HEADS, CHUNK, hd), F32)).astype(BF16)
    cos_d, sin_d = _dft_tables(gd)
    cd = _block_diag(jnp.asarray(np.broadcast_to(cos_d, (FNET_GROUPS, gd, gd)), F32)).astype(BF16)
    sd = _block_diag(jnp.asarray(np.broadcast_to(sin_d, (FNET_GROUPS, gd, gd)), F32)).astype(BF16)
    wc, wsn = _seq_dft_mats(S, 1.0 / math.sqrt(S * gd))

    for l in range(L):
        ws_cat = jnp.transpose(gmlp_ws[l], (1, 0, 2)).reshape(CHUNK, GMLP_HEADS * CHUNK).astype(BF16)
        bs_full = jnp.repeat(gmlp_bs[l].T, hd, axis=1)
        ya, fc, fs = _mixer_in_call(x, mod[l], norm1_g[l][None], w_in[l].astype(BF16), avg, ws_cat, bs_full,
                                    hmask, cd, sd, ts)
        fr = _seq_dft_call(wc, wsn, fc, fs, min(512, S), wf)
        x1, h2, h2t = _mixer_out_call(x, ya, fr, mod[l], _block_diag(fnet_w[l]).astype(BF16),
                                 fnet_b[l].reshape(1, wf), gain_a[l][None], gain_b[l][None],
                                 w_out[l].astype(BF16), norm2_g[l][None], ts)
        h2 = h2.reshape(N, D)
        t1, e1, s2, w2 = _peer_scores_call(h2, peer_wq[l].astype(BF16), peer_k1[l].astype(BF16),
                                           peer_k2[l].astype(BF16), tt)
        x = _peer_dense_call(h2t, t1, e1, s2, w2, peer_u[l].astype(BF16), peer_v[l].T.astype(BF16),
                             x1.reshape(N, D), mod[l], final_g[None], tt, eb, S, l == L - 1).reshape(B, S, D)
    return x
```

```python
import functools
import math

import numpy as np
import jax
import jax.numpy as jnp
from jax import lax
from jax.experimental import pallas as pl
from jax.experimental.pallas import tpu as pltpu

GMLP_HEADS = 8
FNET_GROUPS = 8
CHUNK = 128
PEER_HEADS = 8
PEER_KEYS = 128
PEER_TOPK = 16
N_MOD = 6
EPS = 1e-6

VMEM_LIMIT_BYTES = 56 * 1024 * 1024

F32 = jnp.float32
BF16 = jnp.bfloat16
NEG_INF = float("-inf")


def _params(*sem):
    return pltpu.CompilerParams(dimension_semantics=sem, vmem_limit_bytes=VMEM_LIMIT_BYTES)


def _gelu(x):
    return 0.5 * x * (1.0 + lax.erf(x * (1.0 / math.sqrt(2.0))))


def _rms(x, g):
    return x * lax.rsqrt(jnp.mean(x * x, axis=-1, keepdims=True) + EPS) * g


def _mod_kernel(c_ref, w_ref, b_ref, o_ref):
    c = c_ref[...]
    sc = c * jax.nn.sigmoid(c)
    o_ref[0] = jnp.dot(sc, w_ref[0], preferred_element_type=F32) + b_ref[0]


def _mod_call(c, mod_w, mod_b):
    L, D, M = mod_w.shape
    B = c.shape[0]
    tn = 1536
    return pl.pallas_call(
        _mod_kernel,
        grid=(L, M // tn),
        in_specs=[
            pl.BlockSpec((B, D), lambda l, j: (0, 0)),
            pl.BlockSpec((1, D, tn), lambda l, j: (l, 0, j)),
            pl.BlockSpec((1, 1, tn), lambda l, j: (l, 0, j)),
        ],
        out_specs=pl.BlockSpec((1, B, tn), lambda l, j: (l, 0, j)),
        out_shape=jax.ShapeDtypeStruct((L, B, M), F32),
        compiler_params=_params("parallel", "parallel"),
        name="adaln_mod",
    )(c, mod_w, mod_b.reshape(L, 1, M))


def _mixer_in_kernel(x_ref, mod_ref, g_ref, win_ref, avg_ref, ws_ref, bs_ref, hmask_ref, cd_ref, sd_ref,
                     ya_ref, fc_ref, fs_ref):
    ts = x_ref.shape[1]
    wa = ya_ref.shape[2]
    x = x_ref[0]
    h = _rms(x, g_ref[...]) * (1.0 + mod_ref[0, 1:2, :]) + mod_ref[0, 0:1, :]
    z = jnp.dot(h.astype(BF16), win_ref[...], preferred_element_type=F32)
    u = _gelu(z[:, :wa])
    gv = _gelu(z[:, wa:2 * wa])
    zf = z[:, 2 * wa:].astype(BF16)
    mu = jnp.dot(gv.astype(BF16), avg_ref[...], preferred_element_type=F32)
    dv = gv - mu
    var = jnp.dot((dv * dv).astype(BF16), avg_ref[...], preferred_element_type=F32)
    v = (dv * lax.rsqrt(var + EPS)).astype(BF16)
    hmask = hmask_ref[...]
    for ci in range(ts // CHUNK):
        vc = v[ci * CHUNK:(ci + 1) * CHUNK, :]
        vstack = jnp.concatenate([vc] * GMLP_HEADS, axis=0) * hmask
        sv = jnp.dot(ws_ref[...], vstack, preferred_element_type=F32) + bs_ref[...]
        ya_ref[0, ci * CHUNK:(ci + 1) * CHUNK, :] = u[ci * CHUNK:(ci + 1) * CHUNK, :] * sv
    fc_ref[...] = jnp.dot(zf, cd_ref[...], preferred_element_type=F32).astype(BF16)
    fs_ref[...] = jnp.dot(zf, sd_ref[...], preferred_element_type=F32).astype(BF16)


def _mixer_in_call(x, mod, g, win, avg, ws_cat, bs_full, hmask, cd, sd, ts):
    B, S, D = x.shape
    wa = avg.shape[0]
    wf = cd.shape[0]
    const = lambda shape: pl.BlockSpec(shape, lambda b, s: (0,) * len(shape))
    return pl.pallas_call(
        _mixer_in_kernel,
        grid=(B, S // ts),
        in_specs=[
            pl.BlockSpec((1, ts, D), lambda b, s: (b, s, 0)),
            pl.BlockSpec((1, N_MOD, D), lambda b, s: (b, 0, 0)),
            const(g.shape), const(win.shape), const(avg.shape), const(ws_cat.shape),
            const(bs_full.shape), const(hmask.shape), const(cd.shape), const(sd.shape),
        ],
        out_specs=[
            pl.BlockSpec((1, ts, wa), lambda b, s: (b, s, 0)),
            pl.BlockSpec((ts, wf), lambda b, s: (s, b)),
            pl.BlockSpec((ts, wf), lambda b, s: (s, b)),
        ],
        out_shape=[
            jax.ShapeDtypeStruct((B, S, wa), F32),
            jax.ShapeDtypeStruct((S, B * wf), BF16),
            jax.ShapeDtypeStruct((S, B * wf), BF16),
        ],
        compiler_params=_params("parallel", "parallel"),
        name="mixer_in",
    )(x, mod, g, win, avg, ws_cat, bs_full, hmask, cd, sd)


def _seq_dft_kernel(wc_ref, wsn_ref, fc_ref, fs_ref, o_ref):
    acc = jnp.dot(wc_ref[...], fc_ref[...], preferred_element_type=F32)
    acc += jnp.dot(wsn_ref[...], fs_ref[...], preferred_element_type=F32)
    o_ref[...] = acc.astype(o_ref.dtype)


def _seq_dft_call(wc, wsn, fc, fs, tm, tn):
    S = wc.shape[0]
    ncol = fc.shape[1]
    return pl.pallas_call(
        _seq_dft_kernel,
        grid=(S // tm, ncol // tn),
        in_specs=[
            pl.BlockSpec((tm, S), lambda i, j: (i, 0)),
            pl.BlockSpec((tm, S), lambda i, j: (i, 0)),
            pl.BlockSpec((S, tn), lambda i, j: (0, j)),
            pl.BlockSpec((S, tn), lambda i, j: (0, j)),
        ],
        out_specs=pl.BlockSpec((tm, tn), lambda i, j: (i, j)),
        out_shape=jax.ShapeDtypeStruct((S, ncol), BF16),
        compiler_params=_params("parallel", "parallel"),
        name="seq_dft",
    )(wc, wsn, fc, fs)


def _mixer_out_kernel(x_ref, ya_ref, fr_ref, mod_ref, fw_ref, fb_ref, ga_ref, gb_ref, wout_ref, g2_ref,
                      x1_ref, h2_ref, h2t_ref):
    wa = ya_ref.shape[2]
    yb = jnp.dot(fr_ref[...], fw_ref[...], preferred_element_type=F32) + fb_ref[...]
    na = _rms(ya_ref[0], ga_ref[...]).astype(BF16)
    nb = _rms(yb, gb_ref[...]).astype(BF16)
    mix = jnp.dot(na, wout_ref[:wa, :], preferred_element_type=F32)
    mix += jnp.dot(nb, wout_ref[wa:, :], preferred_element_type=F32)
    x1 = x_ref[0] + mod_ref[0, 2:3, :] * mix
    x1_ref[0] = x1
    h2 = _rms(x1, g2_ref[...]) * (1.0 + mod_ref[0, 4:5, :]) + mod_ref[0, 3:4, :]
    h2_ref[0] = h2.astype(BF16)
    h2t_ref[...] = h2.T.astype(BF16)


def _mixer_out_call(x, ya, fr, mod, fw_bd, fb, ga, gb, wout, g2, ts):
    B, S, D = x.shape
    wa = ya.shape[2]
    wf = fw_bd.shape[0]
    const = lambda shape: pl.BlockSpec(shape, lambda b, s: (0,) * len(shape))
    return pl.pallas_call(
        _mixer_out_kernel,
        grid=(B, S // ts),
        in_specs=[
            pl.BlockSpec((1, ts, D), lambda b, s: (b, s, 0)),
            pl.BlockSpec((1, ts, wa), lambda b, s: (b, s, 0)),
            pl.BlockSpec((ts, wf), lambda b, s: (s, b)),
            pl.BlockSpec((1, N_MOD, D), lambda b, s: (b, 0, 0)),
            const(fw_bd.shape), const(fb.shape), const(ga.shape), const(gb.shape),
            const(wout.shape), const(g2.shape),
        ],
        out_specs=[
            pl.BlockSpec((1, ts, D), lambda b, s: (b, s, 0)),
            pl.BlockSpec((1, ts, D), lambda b, s: (b, s, 0)),
            pl.BlockSpec((D, ts), lambda b, s: (0, b * (S // ts) + s)),
        ],
        out_shape=[
            jax.ShapeDtypeStruct((B, S, D), F32),
            jax.ShapeDtypeStruct((B, S, D), BF16),
            jax.ShapeDtypeStruct((D, B * S), BF16),
        ],
        compiler_params=_params("parallel", "parallel"),
        name="mixer_out",
    )(x, ya, fr, mod, fw_bd, fb, ga, gb, wout, g2)


def _sort16_pairs():
    pairs = []

    def merge(lo, n, r):
        m = 2 * r
        if m < n:
            merge(lo, n, m)
            merge(lo + r, n, m)
            pairs.extend((i, i + r) for i in range(lo + r, lo + n - r, m))
        else:
            pairs.append((lo, lo + r))

    def sort(lo, n):
        if n > 1:
            sort(lo, n // 2)
            sort(lo + n // 2, n // 2)
            merge(lo, n, 1)

    sort(0, 16)
    return pairs


_SORT16 = _sort16_pairs()
_BITONIC16 = [(i, i + s) for s in (8, 4, 2, 1) for i in range(16) if (i // s) % 2 == 0]


def _exchange(v, pairs):
    v = list(v)
    for i, j in pairs:
        v[i], v[j] = jnp.maximum(v[i], v[j]), jnp.minimum(v[i], v[j])
    return v


def _top16(vals):
    groups = [(_exchange(vals[g:g + 16], _SORT16), None) for g in range(0, len(vals), 16)]
    while len(groups) > 1:
        merged = []
        for (a, da), (b, db) in zip(groups[0::2], groups[1::2]):
            keep = [jnp.maximum(a[i], b[15 - i]) for i in range(16)]
            drop = [jnp.minimum(a[i], b[15 - i]) for i in range(16)] + [d for d in (da, db) if d is not None]
            while len(drop) > 1:
                drop = [jnp.maximum(x, y) for x, y in zip(drop[0::2], drop[1::2])] + drop[len(drop) & ~1:]
            merged.append((_exchange(keep, _BITONIC16), drop[0]))
        groups = merged
    return groups[0]


def _peer_scores_kernel(h2_ref, wq_ref, a1_ref, a2_ref, k2_ref, t1_ref, e1_ref, s2_ref, w2_ref,
                        s1_scr, s2_scr, s2h_scr):
    tt = h2_ref.shape[0]
    H, K, half = k2_ref.shape
    LG = 128
    nl = tt // LG
    nt = (((1,), (1,)), ((), ()))
    q = jnp.dot(h2_ref[...], wq_ref[...], preferred_element_type=F32).astype(BF16)
    q1 = jnp.concatenate([q[:, (2 * h) * half:(2 * h + 1) * half] for h in range(H)], axis=1)
    q2 = jnp.concatenate([q[:, (2 * h + 1) * half:(2 * h + 2) * half] for h in range(H)], axis=1)
    s1 = lax.dot_general(a1_ref[...], q1, nt, preferred_element_type=F32)
    s2 = lax.dot_general(a2_ref[...], q2, nt, preferred_element_type=F32)
    for li in range(nl):
        s1_scr[li] = s1[:, li * LG:(li + 1) * LG]
        s2_scr[li] = s2[:, li * LG:(li + 1) * LG]
    for h in range(H):
        sh = lax.dot_general(k2_ref[h], q[:, (2 * h + 1) * half:(2 * h + 2) * half], nt,
                             preferred_element_type=F32)
        for li in range(nl):
            s2h_scr[h, li] = sh[:, li * LG:(li + 1) * LG]

    def lane_group(li, carry):
        v1 = [s1_scr[li, k * H:(k + 1) * H, :] for k in range(K)]
        v2 = [s2_scr[li, k * H:(k + 1) * H, :] for k in range(K)]
        r1, d1 = _top16(v1)
        r2, d2 = _top16(v2)
        r1, r2 = r1 + [d1], r2 + [d2]
        cand = [r1[j] + r2[l] for j in range(17) for l in range(17) if (j + 1) * (l + 1) <= 17]
        pad = [jnp.full((H, LG), NEG_INF, F32)] * (-len(cand) % 16)
        top, c17 = _top16(cand + pad)
        mid = 0.5 * (top[PEER_TOPK - 1] + c17)
        m = r1[0] + r2[0]
        z = None
        for c in cand:
            term = jnp.where(c > mid, jnp.exp(c - m), 0.0)
            z = term if z is None else z + term
        scale = 0.5 / z
        for k in range(K):
            t1_ref[li, k] = mid - v1[k]
            e1_ref[li, k] = jnp.exp(v1[k] - r1[0]) * scale
        for h in range(H):
            sh = s2h_scr[h, li]
            s2_ref[h, li] = sh
            w2_ref[h, li] = jnp.exp(sh - r2[0][h:h + 1, :])
        return carry

    lax.fori_loop(0, nl, lane_group, 0)


def _peer_scores_call(h2, wq, a1, a2, k2, tt):
    N, D = h2.shape
    H, K, half = k2.shape
    LG = 128
    nl = tt // LG
    const = lambda shape: pl.BlockSpec(shape, lambda i: (0,) * len(shape))
    row_spec = pl.BlockSpec((nl, K, H, LG), lambda i: (i, 0, 0, 0))
    row_shape = jax.ShapeDtypeStruct((N // LG, K, H, LG), F32)
    key_spec = pl.BlockSpec((H, nl, K, LG), lambda i: (0, i, 0, 0))
    key_shape = jax.ShapeDtypeStruct((H, N // LG, K, LG), F32)
    return pl.pallas_call(
        _peer_scores_kernel,
        grid=(N // tt,),
        in_specs=[pl.BlockSpec((tt, D), lambda i: (i, 0)), const(wq.shape), const(a1.shape), const(a2.shape),
                  const(k2.shape)],
        out_specs=[row_spec, row_spec, key_spec, key_spec],
        out_shape=[row_shape, row_shape, key_shape, key_shape],
        scratch_shapes=[pltpu.VMEM((nl, K * H, LG), F32), pltpu.VMEM((nl, K * H, LG), F32),
                        pltpu.VMEM((H, nl, K, LG), F32)],
        compiler_params=_params("parallel"),
        name="peer_scores",
    )(h2, wq, a1, a2, k2)


def _peer_dense_kernel(h2t_ref, t1_ref, e1_ref, s2_ref, w2_ref, u_ref, vt_ref, x1_ref, mod_ref, fg_ref,
                       o_ref, acc_ref, p0_ref, p1_ref, c0_ref, c1_ref,
                       us_ref, vts_ref, t1s_ref, e1s_ref, s2s_ref, w2s_ref, h2ts_ref, *, final, nb):
    g = pl.program_id(0)
    tt = h2t_ref.shape[1]
    eb = u_ref.shape[0]
    K = PEER_KEYS
    LG = 128

    @pl.when(g == 0)
    def _():
        acc_ref[...] = jnp.zeros_like(acc_ref)
        p1_ref[...] = jnp.zeros_like(p1_ref)
        c0_ref[...] = jnp.zeros_like(c0_ref)

    us_ref[...] = u_ref[...]
    vts_ref[...] = vt_ref[...]
    t1s_ref[...] = t1_ref[...]
    e1s_ref[...] = e1_ref[...]

    @pl.when(g % nb == 0)
    def _():
        h2ts_ref[...] = h2t_ref[...]

    @pl.when(jnp.logical_or(g == 0, (g - 1) % nb == 0))
    def _():
        s2s_ref[...] = s2_ref[...]
        w2s_ref[...] = w2_ref[...]

    def stages(p_new, p_old, c_new, c_old):
        nrow = eb // K
        dc = acc_ref.shape[0] // nrow
        QR = 32
        nq = K // QR
        units = [(li, q) for li in range(tt // LG) for q in range(nq)]
        assert len(units) == 2 * nrow
        for ui, (li, q) in enumerate(units):
            piece = ui // 2
            if ui % 2 == 0:
                rs = slice(piece * K, (piece + 1) * K)
                pre = jnp.dot(us_ref[rs, :], h2ts_ref[...], preferred_element_type=F32)
                for lj in range(tt // LG):
                    p_new[lj, rs, :] = pre[:, lj * LG:(lj + 1) * LG]
            else:
                ds = slice(piece * dc, (piece + 1) * dc)
                acc_ref[ds, :] += jnp.dot(vts_ref[ds, :], c_old[...], preferred_element_type=F32)
            ls = slice(li * LG, (li + 1) * LG)
            qs = slice(q * QR, (q + 1) * QR)
            gate = [None] * nrow
            for h in range(PEER_HEADS):
                s2 = s2s_ref[h, li, qs, :]
                w2 = w2s_ref[h, li, qs, :]
                for ri in range(nrow):
                    term = jnp.where(s2 > t1s_ref[li, ri, h:h + 1, :], w2, 0.0) * e1s_ref[li, ri, h:h + 1, :]
                    gate[ri] = term if gate[ri] is None else gate[ri] + term
            for ri in range(nrow):
                p = p_old[li, ri * K + q * QR:ri * K + (q + 1) * QR, :]
                coef = gate[ri] * (p * (1.0 + lax.erf(p * (1.0 / math.sqrt(2.0)))))
                c_new[ri * K + q * QR:ri * K + (q + 1) * QR, ls] = coef.astype(BF16)

    @pl.when(g % 2 == 0)
    def _():
        stages(p0_ref, p1_ref, c1_ref, c0_ref)

    @pl.when(g % 2 == 1)
    def _():
        stages(p1_ref, p0_ref, c0_ref, c1_ref)

    @pl.when(jnp.logical_and(g >= 2, (g - 2) % nb == nb - 1))
    def _():
        x2 = x1_ref[...] + mod_ref[0, 5:6, :] * acc_ref[...].T
        if final:
            x2 = _rms(x2, fg_ref[...])
        o_ref[...] = x2
        acc_ref[...] = jnp.zeros_like(acc_ref)


def _peer_dense_call(h2t, t1, e1, s2, w2, u, vt, x1, mod, fg, tt, eb, tokens_per_batch, final):
    D, N = h2t.shape
    H, _, K, LG = s2.shape
    E = u.shape[0]
    nb = E // eb
    total = (N // tt) * nb
    tpb = tokens_per_batch // tt
    rows = eb // K
    blk = lambda g, lag: jnp.clip(g - lag, 0, total - 1)
    row_spec = pl.BlockSpec((tt // LG, rows, H, LG), lambda g: (blk(g, 1) // nb, blk(g, 1) % nb, 0, 0))
    key_spec = pl.BlockSpec((H, tt // LG, K, LG), lambda g: (0, blk(g, 1) // nb, 0, 0))
    return pl.pallas_call(
        functools.partial(_peer_dense_kernel, final=final, nb=nb),
        grid=(total + 2,),
        in_specs=[
            pl.BlockSpec((D, tt), lambda g: (0, blk(g, 0) // nb)),
            row_spec, row_spec, key_spec, key_spec,
            pl.BlockSpec((eb, D), lambda g: (blk(g, 0) % nb, 0)),
            pl.BlockSpec((D, eb), lambda g: (0, blk(g, 2) % nb)),
            pl.BlockSpec((tt, D), lambda g: (blk(g, 2) // nb, 0)),
            pl.BlockSpec((1, N_MOD, D), lambda g: (blk(g, 2) // nb // tpb, 0, 0)),
            pl.BlockSpec((1, D), lambda g: (0, 0)),
        ],
        out_specs=pl.BlockSpec((tt, D), lambda g: (blk(g, 2) // nb, 0)),
        out_shape=jax.ShapeDtypeStruct((N, D), F32),
        scratch_shapes=[pltpu.VMEM((D, tt), F32),
                        pltpu.VMEM((tt // LG, eb, LG), F32), pltpu.VMEM((tt // LG, eb, LG), F32),
                        pltpu.VMEM((eb, tt), BF16), pltpu.VMEM((eb, tt), BF16),
                        pltpu.VMEM((eb, D), BF16), pltpu.VMEM((D, eb), BF16),
                        pltpu.VMEM((tt // LG, rows, H, LG), F32), pltpu.VMEM((tt // LG, rows, H, LG), F32),
                        pltpu.VMEM((H, tt // LG, K, LG), F32), pltpu.VMEM((H, tt // LG, K, LG), F32),
                        pltpu.VMEM((D, tt), BF16)],
        compiler_params=_params("arbitrary"),
        name="peer_dense",
    )(h2t, t1, e1, s2, w2, u, vt, x1, mod, fg)


def _dft_tables(n):
    k = np.arange(n, dtype=np.int64)
    ang = 2.0 * np.pi * ((k[:, None] * k[None, :]) % n) / n
    return np.cos(ang), np.sin(ang)


def _seq_dft_mats(S, scale):
    R = 1 << (int(math.log2(S)) // 2)
    k = jnp.arange(S, dtype=jnp.int32)[:, None]
    w = 2.0 * math.pi / S
    ang_a = ((k * (jnp.arange(S // R, dtype=jnp.int32)[None, :] * R)) % S).astype(F32) * w
    ang_b = ((k * jnp.arange(R, dtype=jnp.int32)[None, :]) % S).astype(F32) * w
    ca, sa = jnp.cos(ang_a)[:, :, None], jnp.sin(ang_a)[:, :, None]
    cb, sb = jnp.cos(ang_b)[:, None, :], jnp.sin(ang_b)[:, None, :]
    cos_w = (ca * cb - sa * sb).reshape(S, S)
    sin_w = (sa * cb + ca * sb).reshape(S, S)
    return (cos_w * scale).astype(BF16), (sin_w * (-scale)).astype(BF16)


def _block_diag(blocks):
    G, a, b = blocks.shape
    eye = jnp.eye(G, dtype=blocks.dtype)
    return (eye[:, None, :, None] * blocks[:, :, None, :]).reshape(G * a, G * b)


def _key_head_rows(keys):
    H, K, d = keys.shape
    eye = jnp.eye(H, dtype=keys.dtype)
    return (jnp.transpose(keys, (1, 0, 2))[:, :, None, :] * eye[None, :, :, None]).reshape(K * H, H * d).astype(BF16)


def kernel(x, c, mod_w, mod_b, norm1_g, w_in, gmlp_ws, gmlp_bs, fnet_w, fnet_b, gain_a, gain_b, w_out,
           norm2_g, peer_wq, peer_k1, peer_k2, peer_u, peer_v, final_g):
    B, S, D = x.shape
    L = mod_w.shape[0]
    wa = gain_a.shape[1]
    wf = gain_b.shape[1]
    hd = wa // GMLP_HEADS
    gd = wf // FNET_GROUPS
    N = B * S
    ts = min(512, S)
    tt = min(512, S)
    eb = 8 * PEER_KEYS

    mod = _mod_call(c, mod_w, mod_b).reshape(L, B, N_MOD, D)

    avg = _block_diag(jnp.full((GMLP_HEADS, hd, hd), 1.0 / hd, F32)).astype(BF16)
    hmask = _block_diag(jnp.ones((GMLP_HEADS, CHUNK, hd), F32)).astype(BF16)
    cos_d, sin_d = _dft_tables(gd)
    cd = _block_diag(jnp.asarray(np.broadcast_to(cos_d, (FNET_GROUPS, gd, gd)), F32)).astype(BF16)
    sd = _block_diag(jnp.asarray(np.broadcast_to(sin_d, (FNET_GROUPS, gd, gd)), F32)).astype(BF16)
    wc, wsn = _seq_dft_mats(S, 1.0 / math.sqrt(S * gd))

    for l in range(L):
        ws_cat = jnp.transpose(gmlp_ws[l], (1, 0, 2)).reshape(CHUNK, GMLP_HEADS * CHUNK).astype(BF16)
        bs_full = jnp.repeat(gmlp_bs[l].T, hd, axis=1)
        ya, fc, fs = _mixer_in_call(x, mod[l], norm1_g[l][None], w_in[l].astype(BF16), avg, ws_cat, bs_full,
                                    hmask, cd, sd, ts)
        fr = _seq_dft_call(wc, wsn, fc, fs, min(512, S), wf)
        x1, h2, h2t = _mixer_out_call(x, ya, fr, mod[l], _block_diag(fnet_w[l]).astype(BF16),
                                 fnet_b[l].reshape(1, wf), gain_a[l][None], gain_b[l][None],
                                 w_out[l].astype(BF16), norm2_g[l][None], ts)
        h2 = h2.reshape(N, D)
        t1, e1, s2, w2 = _peer_scores_call(h2, peer_wq[l].astype(BF16), _key_head_rows(peer_k1[l]),
                                           _key_head_rows(peer_k2[l]), peer_k2[l].astype(BF16), tt)
        x = _peer_dense_call(h2t, t1, e1, s2, w2, peer_u[l].astype(BF16), peer_v[l].T.astype(BF16),
                             x1.reshape(N, D), mod[l], final_g[None], tt, eb, S, l == L - 1).reshape(B, S, D)
    return x
```

```python
import functools
import math

import numpy as np
import jax
import jax.numpy as jnp
from jax import lax
from jax.experimental import pallas as pl
from jax.experimental.pallas import tpu as pltpu

GMLP_HEADS = 8
FNET_GROUPS = 8
CHUNK = 128
PEER_HEADS = 8
PEER_KEYS = 128
PEER_TOPK = 16
N_MOD = 6
EPS = 1e-6

VMEM_LIMIT_BYTES = 56 * 1024 * 1024

F32 = jnp.float32
BF16 = jnp.bfloat16
NEG_INF = float("-inf")


def _params(*sem):
    return pltpu.CompilerParams(dimension_semantics=sem, vmem_limit_bytes=VMEM_LIMIT_BYTES)


def _gelu(x):
    return 0.5 * x * (1.0 + lax.erf(x * (1.0 / math.sqrt(2.0))))


def _rms(x, g):
    return x * lax.rsqrt(jnp.mean(x * x, axis=-1, keepdims=True) + EPS) * g


def _mod_kernel(c_ref, w_ref, b_ref, o_ref):
    c = c_ref[...]
    sc = c * jax.nn.sigmoid(c)
    o_ref[0] = jnp.dot(sc, w_ref[0], preferred_element_type=F32) + b_ref[0]


def _mod_call(c, mod_w, mod_b):
    L, D, M = mod_w.shape
    B = c.shape[0]
    tn = 1536
    return pl.pallas_call(
        _mod_kernel,
        grid=(L, M // tn),
        in_specs=[
            pl.BlockSpec((B, D), lambda l, j: (0, 0)),
            pl.BlockSpec((1, D, tn), lambda l, j: (l, 0, j)),
            pl.BlockSpec((1, 1, tn), lambda l, j: (l, 0, j)),
        ],
        out_specs=pl.BlockSpec((1, B, tn), lambda l, j: (l, 0, j)),
        out_shape=jax.ShapeDtypeStruct((L, B, M), F32),
        compiler_params=_params("parallel", "parallel"),
        name="adaln_mod",
    )(c, mod_w, mod_b.reshape(L, 1, M))


def _mixer_in_kernel(x_ref, mod_ref, g_ref, win_ref, avg_ref, ws_ref, bs_ref, hmask_ref, cd_ref, sd_ref,
                     ya_ref, fc_ref, fs_ref):
    ts = x_ref.shape[1]
    wa = ya_ref.shape[2]
    x = x_ref[0]
    h = _rms(x, g_ref[...]) * (1.0 + mod_ref[0, 1:2, :]) + mod_ref[0, 0:1, :]
    z = jnp.dot(h.astype(BF16), win_ref[...], preferred_element_type=F32)
    u = _gelu(z[:, :wa])
    gv = _gelu(z[:, wa:2 * wa])
    zf = z[:, 2 * wa:].astype(BF16)
    mu = jnp.dot(gv.astype(BF16), avg_ref[...], preferred_element_type=F32)
    dv = gv - mu
    var = jnp.dot((dv * dv).astype(BF16), avg_ref[...], preferred_element_type=F32)
    v = (dv * lax.rsqrt(var + EPS)).astype(BF16)
    hmask = hmask_ref[...]
    for ci in range(ts // CHUNK):
        vc = v[ci * CHUNK:(ci + 1) * CHUNK, :]
        vstack = jnp.concatenate([vc] * GMLP_HEADS, axis=0) * hmask
        sv = jnp.dot(ws_ref[...], vstack, preferred_element_type=F32) + bs_ref[...]
        ya_ref[0, ci * CHUNK:(ci + 1) * CHUNK, :] = u[ci * CHUNK:(ci + 1) * CHUNK, :] * sv
    fc_ref[...] = jnp.dot(zf, cd_ref[...], preferred_element_type=F32).astype(BF16)
    fs_ref[...] = jnp.dot(zf, sd_ref[...], preferred_element_type=F32).astype(BF16)


def _mixer_in_call(x, mod, g, win, avg, ws_cat, bs_full, hmask, cd, sd, ts):
    B, S, D = x.shape
    wa = avg.shape[0]
    wf = cd.shape[0]
    const = lambda shape: pl.BlockSpec(shape, lambda b, s: (0,) * len(shape))
    return pl.pallas_call(
        _mixer_in_kernel,
        grid=(B, S // ts),
        in_specs=[
            pl.BlockSpec((1, ts, D), lambda b, s: (b, s, 0)),
            pl.BlockSpec((1, N_MOD, D), lambda b, s: (b, 0, 0)),
            const(g.shape), const(win.shape), const(avg.shape), const(ws_cat.shape),
            const(bs_full.shape), const(hmask.shape), const(cd.shape), const(sd.shape),
        ],
        out_specs=[
            pl.BlockSpec((1, ts, wa), lambda b, s: (b, s, 0)),
            pl.BlockSpec((ts, wf), lambda b, s: (s, b)),
            pl.BlockSpec((ts, wf), lambda b, s: (s, b)),
        ],
        out_shape=[
            jax.ShapeDtypeStruct((B, S, wa), F32),
            jax.ShapeDtypeStruct((S, B * wf), BF16),
            jax.ShapeDtypeStruct((S, B * wf), BF16),
        ],
        compiler_params=_params("parallel", "parallel"),
        name="mixer_in",
    )(x, mod, g, win, avg, ws_cat, bs_full, hmask, cd, sd)


def _seq_dft_kernel(wc_ref, wsn_ref, fc_ref, fs_ref, o_ref):
    acc = jnp.dot(wc_ref[...], fc_ref[...], preferred_element_type=F32)
    acc += jnp.dot(wsn_ref[...], fs_ref[...], preferred_element_type=F32)
    o_ref[...] = acc.astype(o_ref.dtype)


def _seq_dft_call(wc, wsn, fc, fs, tm, tn):
    S = wc.shape[0]
    ncol = fc.shape[1]
    return pl.pallas_call(
        _seq_dft_kernel,
        grid=(S // tm, ncol // tn),
        in_specs=[
            pl.BlockSpec((tm, S), lambda i, j: (i, 0)),
            pl.BlockSpec((tm, S), lambda i, j: (i, 0)),
            pl.BlockSpec((S, tn), lambda i, j: (0, j)),
            pl.BlockSpec((S, tn), lambda i, j: (0, j)),
        ],
        out_specs=pl.BlockSpec((tm, tn), lambda i, j: (i, j)),
        out_shape=jax.ShapeDtypeStruct((S, ncol), BF16),
        compiler_params=_params("parallel", "parallel"),
        name="seq_dft",
    )(wc, wsn, fc, fs)


def _mixer_out_kernel(x_ref, ya_ref, fr_ref, mod_ref, fw_ref, fb_ref, ga_ref, gb_ref, wout_ref, g2_ref,
                      x1_ref, h2_ref, h2t_ref):
    wa = ya_ref.shape[2]
    yb = jnp.dot(fr_ref[...], fw_ref[...], preferred_element_type=F32) + fb_ref[...]
    na = _rms(ya_ref[0], ga_ref[...]).astype(BF16)
    nb = _rms(yb, gb_ref[...]).astype(BF16)
    mix = jnp.dot(na, wout_ref[:wa, :], preferred_element_type=F32)
    mix += jnp.dot(nb, wout_ref[wa:, :], preferred_element_type=F32)
    x1 = x_ref[0] + mod_ref[0, 2:3, :] * mix
    x1_ref[0] = x1
    h2 = _rms(x1, g2_ref[...]) * (1.0 + mod_ref[0, 4:5, :]) + mod_ref[0, 3:4, :]
    h2_ref[0] = h2.astype(BF16)
    h2t_ref[...] = h2.T.astype(BF16)


def _mixer_out_call(x, ya, fr, mod, fw_bd, fb, ga, gb, wout, g2, ts):
    B, S, D = x.shape
    wa = ya.shape[2]
    wf = fw_bd.shape[0]
    const = lambda shape: pl.BlockSpec(shape, lambda b, s: (0,) * len(shape))
    return pl.pallas_call(
        _mixer_out_kernel,
        grid=(B, S // ts),
        in_specs=[
            pl.BlockSpec((1, ts, D), lambda b, s: (b, s, 0)),
            pl.BlockSpec((1, ts, wa), lambda b, s: (b, s, 0)),
            pl.BlockSpec((ts, wf), lambda b, s: (s, b)),
            pl.BlockSpec((1, N_MOD, D), lambda b, s: (b, 0, 0)),
            const(fw_bd.shape), const(fb.shape), const(ga.shape), const(gb.shape),
            const(wout.shape), const(g2.shape),
        ],
        out_specs=[
            pl.BlockSpec((1, ts, D), lambda b, s: (b, s, 0)),
            pl.BlockSpec((1, ts, D), lambda b, s: (b, s, 0)),
            pl.BlockSpec((D, ts), lambda b, s: (0, b * (S // ts) + s)),
        ],
        out_shape=[
            jax.ShapeDtypeStruct((B, S, D), F32),
            jax.ShapeDtypeStruct((B, S, D), BF16),
            jax.ShapeDtypeStruct((D, B * S), BF16),
        ],
        compiler_params=_params("parallel", "parallel"),
        name="mixer_out",
    )(x, ya, fr, mod, fw_bd, fb, ga, gb, wout, g2)


def _sort16_pairs():
    pairs = []

    def merge(lo, n, r):
        m = 2 * r
        if m < n:
            merge(lo, n, m)
            merge(lo + r, n, m)
            pairs.extend((i, i + r) for i in range(lo + r, lo + n - r, m))
        else:
            pairs.append((lo, lo + r))

    def sort(lo, n):
        if n > 1:
            sort(lo, n // 2)
            sort(lo + n // 2, n // 2)
            merge(lo, n, 1)

    sort(0, 16)
    return pairs


_SORT16 = _sort16_pairs()
_BITONIC16 = [(i, i + s) for s in (8, 4, 2, 1) for i in range(16) if (i // s) % 2 == 0]


def _exchange(v, pairs):
    v = list(v)
    for i, j in pairs:
        v[i], v[j] = jnp.maximum(v[i], v[j]), jnp.minimum(v[i], v[j])
    return v


def _top16(vals):
    groups = [(_exchange(vals[g:g + 16], _SORT16), None) for g in range(0, len(vals), 16)]
    while len(groups) > 1:
        merged = []
        for (a, da), (b, db) in zip(groups[0::2], groups[1::2]):
            keep = [jnp.maximum(a[i], b[15 - i]) for i in range(16)]
            drop = [jnp.minimum(a[i], b[15 - i]) for i in range(16)] + [d for d in (da, db) if d is not None]
            while len(drop) > 1:
                drop = [jnp.maximum(x, y) for x, y in zip(drop[0::2], drop[1::2])] + drop[len(drop) & ~1:]
            merged.append((_exchange(keep, _BITONIC16), drop[0]))
        groups = merged
    return groups[0]


def _bf16_pair(x):
    bits = lax.bitcast_convert_type(x.astype(BF16).astype(F32), jnp.uint32)
    return bits | (bits >> 16)


def _peer_scores_kernel(h2_ref, wq_ref, a1_ref, a2_ref, k2_ref, n_ref, e1_ref, rank_ref, w2_ref,
                        s1_scr, s2_scr, s2h_scr):
    tt = h2_ref.shape[0]
    H, K, half = k2_ref.shape
    LG = 128
    nl = tt // LG
    nt = (((1,), (1,)), ((), ()))
    q = jnp.dot(h2_ref[...], wq_ref[...], preferred_element_type=F32).astype(BF16)
    q1 = jnp.concatenate([q[:, (2 * h) * half:(2 * h + 1) * half] for h in range(H)], axis=1)
    q2 = jnp.concatenate([q[:, (2 * h + 1) * half:(2 * h + 2) * half] for h in range(H)], axis=1)
    s1 = lax.dot_general(a1_ref[...], q1, nt, preferred_element_type=F32)
    s2 = lax.dot_general(a2_ref[...], q2, nt, preferred_element_type=F32)
    for li in range(nl):
        s1_scr[li] = s1[:, li * LG:(li + 1) * LG]
        s2_scr[li] = s2[:, li * LG:(li + 1) * LG]
    for h in range(H):
        sh = lax.dot_general(k2_ref[h], q[:, (2 * h + 1) * half:(2 * h + 2) * half], nt,
                             preferred_element_type=F32)
        for li in range(nl):
            s2h_scr[h, li] = sh[:, li * LG:(li + 1) * LG]

    def lane_group(li, carry):
        v1 = [s1_scr[li, k * H:(k + 1) * H, :] for k in range(K)]
        v2 = [s2_scr[li, k * H:(k + 1) * H, :] for k in range(K)]
        r1, d1 = _top16(v1)
        r2, d2 = _top16(v2)
        r1, r2 = r1 + [d1], r2 + [d2]
        cand = [r1[j] + r2[l] for j in range(17) for l in range(17) if (j + 1) * (l + 1) <= 17]
        pad = [jnp.full((H, LG), NEG_INF, F32)] * (-len(cand) % 16)
        top, c17 = _top16(cand + pad)
        mid = 0.5 * (top[PEER_TOPK - 1] + c17)
        m = r1[0] + r2[0]
        z = None
        for c in cand:
            term = jnp.where(c > mid, jnp.exp(c - m), 0.0)
            z = term if z is None else z + term
        scale = 0.5 / z
        need = [mid - r2[l] for l in range(PEER_TOPK)]
        for k in range(K):
            n = jnp.zeros((H, LG), F32)
            for l in range(PEER_TOPK):
                n = jnp.where(v1[k] > need[l], float(l + 1), n)
            n_ref[li, k] = _bf16_pair(n)
            e1_ref[li, k] = _bf16_pair(jnp.exp(v1[k] - r1[0]) * scale)
        cut = [0.5 * (r2[l] + r2[l + 1]) for l in range(PEER_TOPK)]
        for h in range(H):
            sh = s2h_scr[h, li]
            rank = jnp.full((K, LG), float(PEER_TOPK), F32)
            for l in reversed(range(PEER_TOPK)):
                rank = jnp.where(sh > cut[l][h:h + 1, :], float(l), rank)
            rank_ref[h, li] = rank.astype(BF16)
            w2_ref[h, li] = jnp.exp(sh - r2[0][h:h + 1, :]).astype(BF16)
        return carry

    lax.fori_loop(0, nl, lane_group, 0)


def _peer_scores_call(h2, wq, a1, a2, k2, tt):
    N, D = h2.shape
    H, K, half = k2.shape
    LG = 128
    nl = tt // LG
    const = lambda shape: pl.BlockSpec(shape, lambda i: (0,) * len(shape))
    row_spec = pl.BlockSpec((nl, K, H, LG), lambda i: (i, 0, 0, 0))
    row_shape = jax.ShapeDtypeStruct((N // LG, K, H, LG), jnp.uint32)
    key_spec = pl.BlockSpec((H, nl, K, LG), lambda i: (0, i, 0, 0))
    key_shape = jax.ShapeDtypeStruct((H, N // LG, K, LG), BF16)
    return pl.pallas_call(
        _peer_scores_kernel,
        grid=(N // tt,),
        in_specs=[pl.BlockSpec((tt, D), lambda i: (i, 0)), const(wq.shape), const(a1.shape), const(a2.shape),
                  const(k2.shape)],
        out_specs=[row_spec, row_spec, key_spec, key_spec],
        out_shape=[row_shape, row_shape, key_shape, key_shape],
        scratch_shapes=[pltpu.VMEM((nl, K * H, LG), F32), pltpu.VMEM((nl, K * H, LG), F32),
                        pltpu.VMEM((H, nl, K, LG), F32)],
        compiler_params=_params("parallel"),
        name="peer_scores",
    )(h2, wq, a1, a2, k2)


def _peer_dense_kernel(h2t_ref, n_ref, e1_ref, rank_ref, w2_ref, u_ref, vt_ref, x1_ref, mod_ref, fg_ref,
                       o_ref, acc_ref, p0_ref, p1_ref, c0_ref, c1_ref,
                       us_ref, vts_ref, ns_ref, e1s_ref, ranks_ref, w2s_ref, h2ts_ref, *, final, nb):
    g = pl.program_id(0)
    tt = h2t_ref.shape[1]
    eb = u_ref.shape[0]
    K = PEER_KEYS
    LG = 128

    @pl.when(g == 0)
    def _():
        acc_ref[...] = jnp.zeros_like(acc_ref)
        p1_ref[...] = jnp.zeros_like(p1_ref)
        c0_ref[...] = jnp.zeros_like(c0_ref)

    us_ref[...] = u_ref[...]
    vts_ref[...] = vt_ref[...]
    ns_ref[...] = n_ref[...]
    e1s_ref[...] = e1_ref[...]

    @pl.when(g % nb == 0)
    def _():
        h2ts_ref[...] = h2t_ref[...]

    @pl.when(jnp.logical_or(g == 0, (g - 1) % nb == 0))
    def _():
        ranks_ref[...] = rank_ref[...]
        w2s_ref[...] = w2_ref[...]

    def stages(p_new, p_old, c_new, c_old):
        nrow = eb // K
        dc = acc_ref.shape[0] // nrow
        QR = 32
        nq = K // QR
        units = [(li, q) for li in range(tt // LG) for q in range(nq)]
        assert len(units) == 2 * nrow

        def row_tile(row):
            return pltpu.bitcast(jnp.broadcast_to(row, (QR // 2, LG)), BF16)

        for ui, (li, q) in enumerate(units):
            piece = ui // 2
            if ui % 2 == 0:
                rs = slice(piece * K, (piece + 1) * K)
                pre = jnp.dot(us_ref[rs, :], h2ts_ref[...], preferred_element_type=F32)
                for lj in range(tt // LG):
                    p_new[lj, rs, :] = pre[:, lj * LG:(lj + 1) * LG]
            else:
                ds = slice(piece * dc, (piece + 1) * dc)
                acc_ref[ds, :] += jnp.dot(vts_ref[ds, :], c_old[...], preferred_element_type=F32)
            ls = slice(li * LG, (li + 1) * LG)
            qs = slice(q * QR, (q + 1) * QR)
            gate = [None] * nrow
            for h in range(PEER_HEADS):
                rank = ranks_ref[h, li, qs, :]
                w2 = w2s_ref[h, li, qs, :]
                for ri in range(nrow):
                    count = row_tile(ns_ref[li, ri, h:h + 1, :])
                    term = jnp.where(rank < count, w2, jnp.zeros_like(w2)) * row_tile(e1s_ref[li, ri, h:h + 1, :])
                    gate[ri] = term if gate[ri] is None else gate[ri] + term
            for ri in range(nrow):
                p = p_old[li, ri * K + q * QR:ri * K + (q + 1) * QR, :]
                act = (p * (1.0 + lax.erf(p * (1.0 / math.sqrt(2.0))))).astype(BF16)
                c_new[ri * K + q * QR:ri * K + (q + 1) * QR, ls] = gate[ri] * act

    @pl.when(g % 2 == 0)
    def _():
        stages(p0_ref, p1_ref, c1_ref, c0_ref)

    @pl.when(g % 2 == 1)
    def _():
        stages(p1_ref, p0_ref, c0_ref, c1_ref)

    @pl.when(jnp.logical_and(g >= 2, (g - 2) % nb == nb - 1))
    def _():
        x2 = x1_ref[...] + mod_ref[0, 5:6, :] * acc_ref[...].T
        if final:
            x2 = _rms(x2, fg_ref[...])
        o_ref[...] = x2
        acc_ref[...] = jnp.zeros_like(acc_ref)


def _peer_dense_call(h2t, n, e1, rank, w2, u, vt, x1, mod, fg, tt, eb, tokens_per_batch, final):
    D, N = h2t.shape
    H, _, K, LG = rank.shape
    E = u.shape[0]
    nb = E // eb
    total = (N // tt) * nb
    tpb = tokens_per_batch // tt
    rows = eb // K
    blk = lambda g, lag: jnp.clip(g - lag, 0, total - 1)
    row_spec = pl.BlockSpec((tt // LG, rows, H, LG), lambda g: (blk(g, 1) // nb, blk(g, 1) % nb, 0, 0))
    key_spec = pl.BlockSpec((H, tt // LG, K, LG), lambda g: (0, blk(g, 1) // nb, 0, 0))
    return pl.pallas_call(
        functools.partial(_peer_dense_kernel, final=final, nb=nb),
        grid=(total + 2,),
        in_specs=[
            pl.BlockSpec((D, tt), lambda g: (0, blk(g, 0) // nb)),
            row_spec, row_spec, key_spec, key_spec,
            pl.BlockSpec((eb, D), lambda g: (blk(g, 0) % nb, 0)),
            pl.BlockSpec((D, eb), lambda g: (0, blk(g, 2) % nb)),
            pl.BlockSpec((tt, D), lambda g: (blk(g, 2) // nb, 0)),
            pl.BlockSpec((1, N_MOD, D), lambda g: (blk(g, 2) // nb // tpb, 0, 0)),
            pl.BlockSpec((1, D), lambda g: (0, 0)),
        ],
        out_specs=pl.BlockSpec((tt, D), lambda g: (blk(g, 2) // nb, 0)),
        out_shape=jax.ShapeDtypeStruct((N, D), F32),
        scratch_shapes=[pltpu.VMEM((D, tt), F32),
                        pltpu.VMEM((tt // LG, eb, LG), F32), pltpu.VMEM((tt // LG, eb, LG), F32),
                        pltpu.VMEM((eb, tt), BF16), pltpu.VMEM((eb, tt), BF16),
                        pltpu.VMEM((eb, D), BF16), pltpu.VMEM((D, eb), BF16),
                        pltpu.VMEM((tt // LG, rows, H, LG), jnp.uint32), pltpu.VMEM((tt // LG, rows, H, LG), jnp.uint32),
                        pltpu.VMEM((H, tt // LG, K, LG), BF16), pltpu.VMEM((H, tt // LG, K, LG), BF16),
                        pltpu.VMEM((D, tt), BF16)],
        compiler_params=_params("arbitrary"),
        name="peer_dense",
    )(h2t, n, e1, rank, w2, u, vt, x1, mod, fg)


def _dft_tables(n):
    k = np.arange(n, dtype=np.int64)
    ang = 2.0 * np.pi * ((k[:, None] * k[None, :]) % n) / n
    return np.cos(ang), np.sin(ang)


def _seq_dft_mats(S, scale):
    R = 1 << (int(math.log2(S)) // 2)
    k = jnp.arange(S, dtype=jnp.int32)[:, None]
    w = 2.0 * math.pi / S
    ang_a = ((k * (jnp.arange(S // R, dtype=jnp.int32)[None, :] * R)) % S).astype(F32) * w
    ang_b = ((k * jnp.arange(R, dtype=jnp.int32)[None, :]) % S).astype(F32) * w
    ca, sa = jnp.cos(ang_a)[:, :, None], jnp.sin(ang_a)[:, :, None]
    cb, sb = jnp.cos(ang_b)[:, None, :], jnp.sin(ang_b)[:, None, :]
    cos_w = (ca * cb - sa * sb).reshape(S, S)
    sin_w = (sa * cb + ca * sb).reshape(S, S)
    return (cos_w * scale).astype(BF16), (sin_w * (-scale)).astype(BF16)


def _block_diag(blocks):
    G, a, b = blocks.shape
    eye = jnp.eye(G, dtype=blocks.dtype)
    return (eye[:, None, :, None] * blocks[:, :, None, :]).reshape(G * a, G * b)


def _key_head_rows(keys):
    H, K, d = keys.shape
    eye = jnp.eye(H, dtype=keys.dtype)
    return (jnp.transpose(keys, (1, 0, 2))[:, :, None, :] * eye[None, :, :, None]).reshape(K * H, H * d).astype(BF16)


def kernel(x, c, mod_w, mod_b, norm1_g, w_in, gmlp_ws, gmlp_bs, fnet_w, fnet_b, gain_a, gain_b, w_out,
           norm2_g, peer_wq, peer_k1, peer_k2, peer_u, peer_v, final_g):
    B, S, D = x.shape
    L = mod_w.shape[0]
    wa = gain_a.shape[1]
    wf = gain_b.shape[1]
    hd = wa // GMLP_HEADS
    gd = wf // FNET_GROUPS
    N = B * S
    ts = min(512, S)
    tt = min(512, S)
    eb = 8 * PEER_KEYS

    mod = _mod_call(c, mod_w, mod_b).reshape(L, B, N_MOD, D)

    avg = _block_diag(jnp.full((GMLP_HEADS, hd, hd), 1.0 / hd, F32)).astype(BF16)
    hmask = _block_diag(jnp.ones((GMLP_HEADS, CHUNK, hd), F32)).astype(BF16)
    cos_d, sin_d = _dft_tables(gd)
    cd = _block_diag(jnp.asarray(np.broadcast_to(cos_d, (FNET_GROUPS, gd, gd)), F32)).astype(BF16)
    sd = _block_diag(jnp.asarray(np.broadcast_to(sin_d, (FNET_GROUPS, gd, gd)), F32)).astype(BF16)
    wc, wsn = _seq_dft_mats(S, 1.0 / math.sqrt(S * gd))

    for l in range(L):
        ws_cat = jnp.transpose(gmlp_ws[l], (1, 0, 2)).reshape(CHUNK, GMLP_HEADS * CHUNK).astype(BF16)
        bs_full = jnp.repeat(gmlp_bs[l].T, hd, axis=1)
        ya, fc, fs = _mixer_in_call(x, mod[l], norm1_g[l][None], w_in[l].astype(BF16), avg, ws_cat, bs_full,
                                    hmask, cd, sd, ts)
        fr = _seq_dft_call(wc, wsn, fc, fs, min(512, S), wf)
        x1, h2, h2t = _mixer_out_call(x, ya, fr, mod[l], _block_diag(fnet_w[l]).astype(BF16),
                                 fnet_b[l].reshape(1, wf), gain_a[l][None], gain_b[l][None],
                                 w_out[l].astype(BF16), norm2_g[l][None], ts)
        h2 = h2.reshape(N, D)
        n, e1, rank, w2 = _peer_scores_call(h2, peer_wq[l].astype(BF16), _key_head_rows(peer_k1[l]),
                                           _key_head_rows(peer_k2[l]), peer_k2[l].astype(BF16), tt)
        x = _peer_dense_call(h2t, n, e1, rank, w2, peer_u[l].astype(BF16), peer_v[l].T.astype(BF16),
                             x1.reshape(N, D), mod[l], final_g[None], tt, eb, S, l == L - 1).reshape(B, S, D)
    return x
```

```python
import functools
import math

import numpy as np
import jax
import jax.numpy as jnp
from jax import lax
from jax.experimental import pallas as pl
from jax.experimental.pallas import tpu as pltpu

GMLP_HEADS = 8
FNET_GROUPS = 8
CHUNK = 128
PEER_HEADS = 8
PEER_KEYS = 128
PEER_TOPK = 16
N_MOD = 6
EPS = 1e-6

VMEM_LIMIT_BYTES = 56 * 1024 * 1024

F32 = jnp.float32
BF16 = jnp.bfloat16
NEG_INF = float("-inf")


def _params(*sem):
    return pltpu.CompilerParams(dimension_semantics=sem, vmem_limit_bytes=VMEM_LIMIT_BYTES)


def _gelu(x):
    return 0.5 * x * (1.0 + lax.erf(x * (1.0 / math.sqrt(2.0))))


def _rms(x, g):
    return x * lax.rsqrt(jnp.mean(x * x, axis=-1, keepdims=True) + EPS) * g


def _mod_kernel(c_ref, w_ref, b_ref, o_ref):
    c = c_ref[...]
    sc = c * jax.nn.sigmoid(c)
    o_ref[0] = jnp.dot(sc, w_ref[0], preferred_element_type=F32) + b_ref[0]


def _mod_call(c, mod_w, mod_b):
    L, D, M = mod_w.shape
    B = c.shape[0]
    tn = 1536
    return pl.pallas_call(
        _mod_kernel,
        grid=(L, M // tn),
        in_specs=[
            pl.BlockSpec((B, D), lambda l, j: (0, 0)),
            pl.BlockSpec((1, D, tn), lambda l, j: (l, 0, j)),
            pl.BlockSpec((1, 1, tn), lambda l, j: (l, 0, j)),
        ],
        out_specs=pl.BlockSpec((1, B, tn), lambda l, j: (l, 0, j)),
        out_shape=jax.ShapeDtypeStruct((L, B, M), F32),
        compiler_params=_params("parallel", "parallel"),
        name="adaln_mod",
    )(c, mod_w, mod_b.reshape(L, 1, M))


def _mixer_in_kernel(x_ref, mod_ref, g_ref, win_ref, avg_ref, ws_ref, bs_ref, hmask_ref, cd_ref, sd_ref,
                     ya_ref, fc_ref, fs_ref):
    ts = x_ref.shape[1]
    wa = ya_ref.shape[2]
    x = x_ref[0]
    h = _rms(x, g_ref[...]) * (1.0 + mod_ref[0, 1:2, :]) + mod_ref[0, 0:1, :]
    z = jnp.dot(h.astype(BF16), win_ref[...], preferred_element_type=F32)
    u = _gelu(z[:, :wa])
    gv = _gelu(z[:, wa:2 * wa])
    zf = z[:, 2 * wa:].astype(BF16)
    mu = jnp.dot(gv.astype(BF16), avg_ref[...], preferred_element_type=F32)
    dv = gv - mu
    var = jnp.dot((dv * dv).astype(BF16), avg_ref[...], preferred_element_type=F32)
    v = (dv * lax.rsqrt(var + EPS)).astype(BF16)
    hmask = hmask_ref[...]
    for ci in range(ts // CHUNK):
        vc = v[ci * CHUNK:(ci + 1) * CHUNK, :]
        vstack = jnp.concatenate([vc] * GMLP_HEADS, axis=0) * hmask
        sv = jnp.dot(ws_ref[...], vstack, preferred_element_type=F32) + bs_ref[...]
        ya_ref[0, ci * CHUNK:(ci + 1) * CHUNK, :] = u[ci * CHUNK:(ci + 1) * CHUNK, :] * sv
    fc_ref[...] = jnp.dot(zf, cd_ref[...], preferred_element_type=F32).astype(BF16)
    fs_ref[...] = jnp.dot(zf, sd_ref[...], preferred_element_type=F32).astype(BF16)


def _mixer_in_call(x, mod, g, win, avg, ws_cat, bs_full, hmask, cd, sd, ts):
    B, S, D = x.shape
    wa = avg.shape[0]
    wf = cd.shape[0]
    const = lambda shape: pl.BlockSpec(shape, lambda b, s: (0,) * len(shape))
    return pl.pallas_call(
        _mixer_in_kernel,
        grid=(B, S // ts),
        in_specs=[
            pl.BlockSpec((1, ts, D), lambda b, s: (b, s, 0)),
            pl.BlockSpec((1, N_MOD, D), lambda b, s: (b, 0, 0)),
            const(g.shape), const(win.shape), const(avg.shape), const(ws_cat.shape),
            const(bs_full.shape), const(hmask.shape), const(cd.shape), const(sd.shape),
        ],
        out_specs=[
            pl.BlockSpec((1, ts, wa), lambda b, s: (b, s, 0)),
            pl.BlockSpec((ts, wf), lambda b, s: (s, b)),
            pl.BlockSpec((ts, wf), lambda b, s: (s, b)),
        ],
        out_shape=[
            jax.ShapeDtypeStruct((B, S, wa), F32),
            jax.ShapeDtypeStruct((S, B * wf), BF16),
            jax.ShapeDtypeStruct((S, B * wf), BF16),
        ],
        compiler_params=_params("parallel", "parallel"),
        name="mixer_in",
    )(x, mod, g, win, avg, ws_cat, bs_full, hmask, cd, sd)


def _seq_dft_kernel(wc_ref, wsn_ref, fc_ref, fs_ref, o_ref):
    acc = jnp.dot(wc_ref[...], fc_ref[...], preferred_element_type=F32)
    acc += jnp.dot(wsn_ref[...], fs_ref[...], preferred_element_type=F32)
    o_ref[...] = acc.astype(o_ref.dtype)


def _seq_dft_call(wc, wsn, fc, fs, tm, tn):
    S = wc.shape[0]
    ncol = fc.shape[1]
    return pl.pallas_call(
        _seq_dft_kernel,
        grid=(S // tm, ncol // tn),
        in_specs=[
            pl.BlockSpec((tm, S), lambda i, j: (i, 0)),
            pl.BlockSpec((tm, S), lambda i, j: (i, 0)),
            pl.BlockSpec((S, tn), lambda i, j: (0, j)),
            pl.BlockSpec((S, tn), lambda i, j: (0, j)),
        ],
        out_specs=pl.BlockSpec((tm, tn), lambda i, j: (i, j)),
        out_shape=jax.ShapeDtypeStruct((S, ncol), BF16),
        compiler_params=_params("parallel", "parallel"),
        name="seq_dft",
    )(wc, wsn, fc, fs)


def _mixer_out_kernel(x_ref, ya_ref, fr_ref, mod_ref, fw_ref, fb_ref, ga_ref, gb_ref, wout_ref, g2_ref,
                      x1_ref, h2_ref, h2t_ref):
    wa = ya_ref.shape[2]
    yb = jnp.dot(fr_ref[...], fw_ref[...], preferred_element_type=F32) + fb_ref[...]
    na = _rms(ya_ref[0], ga_ref[...]).astype(BF16)
    nb = _rms(yb, gb_ref[...]).astype(BF16)
    mix = jnp.dot(na, wout_ref[:wa, :], preferred_element_type=F32)
    mix += jnp.dot(nb, wout_ref[wa:, :], preferred_element_type=F32)
    x1 = x_ref[0] + mod_ref[0, 2:3, :] * mix
    x1_ref[0] = x1
    h2 = _rms(x1, g2_ref[...]) * (1.0 + mod_ref[0, 4:5, :]) + mod_ref[0, 3:4, :]
    h2_ref[0] = h2.astype(BF16)
    h2t_ref[...] = h2.T.astype(BF16)


def _mixer_out_call(x, ya, fr, mod, fw_bd, fb, ga, gb, wout, g2, ts):
    B, S, D = x.shape
    wa = ya.shape[2]
    wf = fw_bd.shape[0]
    const = lambda shape: pl.BlockSpec(shape, lambda b, s: (0,) * len(shape))
    return pl.pallas_call(
        _mixer_out_kernel,
        grid=(B, S // ts),
        in_specs=[
            pl.BlockSpec((1, ts, D), lambda b, s: (b, s, 0)),
            pl.BlockSpec((1, ts, wa), lambda b, s: (b, s, 0)),
            pl.BlockSpec((ts, wf), lambda b, s: (s, b)),
            pl.BlockSpec((1, N_MOD, D), lambda b, s: (b, 0, 0)),
            const(fw_bd.shape), const(fb.shape), const(ga.shape), const(gb.shape),
            const(wout.shape), const(g2.shape),
        ],
        out_specs=[
            pl.BlockSpec((1, ts, D), lambda b, s: (b, s, 0)),
            pl.BlockSpec((1, ts, D), lambda b, s: (b, s, 0)),
            pl.BlockSpec((D, ts), lambda b, s: (0, b * (S // ts) + s)),
        ],
        out_shape=[
            jax.ShapeDtypeStruct((B, S, D), F32),
            jax.ShapeDtypeStruct((B, S, D), BF16),
            jax.ShapeDtypeStruct((D, B * S), BF16),
        ],
        compiler_params=_params("parallel", "parallel"),
        name="mixer_out",
    )(x, ya, fr, mod, fw_bd, fb, ga, gb, wout, g2)


def _sort16_pairs():
    pairs = []

    def merge(lo, n, r):
        m = 2 * r
        if m < n:
            merge(lo, n, m)
            merge(lo + r, n, m)
            pairs.extend((i, i + r) for i in range(lo + r, lo + n - r, m))
        else:
            pairs.append((lo, lo + r))

    def sort(lo, n):
        if n > 1:
            sort(lo, n // 2)
            sort(lo + n // 2, n // 2)
            merge(lo, n, 1)

    sort(0, 16)
    return pairs


_SORT16 = _sort16_pairs()
_BITONIC16 = [(i, i + s) for s in (8, 4, 2, 1) for i in range(16) if (i // s) % 2 == 0]


def _exchange(v, pairs):
    v = list(v)
    for i, j in pairs:
        v[i], v[j] = jnp.maximum(v[i], v[j]), jnp.minimum(v[i], v[j])
    return v


def _top16(vals):
    groups = [(_exchange(vals[g:g + 16], _SORT16), None) for g in range(0, len(vals), 16)]
    while len(groups) > 1:
        merged = []
        for (a, da), (b, db) in zip(groups[0::2], groups[1::2]):
            keep = [jnp.maximum(a[i], b[15 - i]) for i in range(16)]
            drop = [jnp.minimum(a[i], b[15 - i]) for i in range(16)] + [d for d in (da, db) if d is not None]
            while len(drop) > 1:
                drop = [jnp.maximum(x, y) for x, y in zip(drop[0::2], drop[1::2])] + drop[len(drop) & ~1:]
            merged.append((_exchange(keep, _BITONIC16), drop[0]))
        groups = merged
    return groups[0]


def _peer_scores_kernel(h2_ref, wq_ref, a1_ref, a2_ref, k2_ref, t1_ref, e1_ref, s2_ref, w2_ref,
                        s1_scr, s2_scr, s2h_scr):
    tt = h2_ref.shape[0]
    H, K, half = k2_ref.shape
    LG = 128
    nl = tt // LG
    nt = (((1,), (1,)), ((), ()))
    q = jnp.dot(h2_ref[...], wq_ref[...], preferred_element_type=F32).astype(BF16)
    q1 = jnp.concatenate([q[:, (2 * h) * half:(2 * h + 1) * half] for h in range(H)], axis=1)
    q2 = jnp.concatenate([q[:, (2 * h + 1) * half:(2 * h + 2) * half] for h in range(H)], axis=1)
    s1 = lax.dot_general(a1_ref[...], q1, nt, preferred_element_type=F32)
    s2 = lax.dot_general(a2_ref[...], q2, nt, preferred_element_type=F32)
    for li in range(nl):
        s1_scr[li] = s1[:, li * LG:(li + 1) * LG]
        s2_scr[li] = s2[:, li * LG:(li + 1) * LG]
    for h in range(H):
        sh = lax.dot_general(k2_ref[h], q[:, (2 * h + 1) * half:(2 * h + 2) * half], nt,
                             preferred_element_type=F32)
        for li in range(nl):
            s2h_scr[h, li] = sh[:, li * LG:(li + 1) * LG]

    def lane_group(li, carry):
        v1 = [s1_scr[li, k * H:(k + 1) * H, :] for k in range(K)]
        v2 = [s2_scr[li, k * H:(k + 1) * H, :] for k in range(K)]
        r1, d1 = _top16(v1)
        r2, d2 = _top16(v2)
        r1, r2 = r1 + [d1], r2 + [d2]
        cand = [r1[j] + r2[l] for j in range(17) for l in range(17) if (j + 1) * (l + 1) <= 17]
        pad = [jnp.full((H, LG), NEG_INF, F32)] * (-len(cand) % 16)
        top, c17 = _top16(cand + pad)
        mid = 0.5 * (top[PEER_TOPK - 1] + c17)
        m = r1[0] + r2[0]
        z = None
        for c in cand:
            term = jnp.where(c > mid, jnp.exp(c - m), 0.0)
            z = term if z is None else z + term
        scale = 0.5 / z
        for k in range(K):
            t1_ref[li, k] = mid - v1[k]
            e1_ref[li, k] = jnp.exp(v1[k] - r1[0]) * scale
        for h in range(H):
            sh = s2h_scr[h, li]
            s2_ref[h, li] = sh
            w2_ref[h, li] = jnp.exp(sh - r2[0][h:h + 1, :])
        return carry

    lax.fori_loop(0, nl, lane_group, 0)


def _peer_scores_call(h2, wq, a1, a2, k2, tt):
    N, D = h2.shape
    H, K, half = k2.shape
    LG = 128
    nl = tt // LG
    const = lambda shape: pl.BlockSpec(shape, lambda i: (0,) * len(shape))
    row_spec = pl.BlockSpec((nl, K, H, LG), lambda i: (i, 0, 0, 0))
    row_shape = jax.ShapeDtypeStruct((N // LG, K, H, LG), F32)
    key_spec = pl.BlockSpec((H, nl, K, LG), lambda i: (0, i, 0, 0))
    key_shape = jax.ShapeDtypeStruct((H, N // LG, K, LG), F32)
    return pl.pallas_call(
        _peer_scores_kernel,
        grid=(N // tt,),
        in_specs=[pl.BlockSpec((tt, D), lambda i: (i, 0)), const(wq.shape), const(a1.shape), const(a2.shape),
                  const(k2.shape)],
        out_specs=[row_spec, row_spec, key_spec, key_spec],
        out_shape=[row_shape, row_shape, key_shape, key_shape],
        scratch_shapes=[pltpu.VMEM((nl, K * H, LG), F32), pltpu.VMEM((nl, K * H, LG), F32),
                        pltpu.VMEM((H, nl, K, LG), F32)],
        compiler_params=_params("parallel"),
        name="peer_scores",
    )(h2, wq, a1, a2, k2)


def _peer_dense_kernel(h2t_ref, t1_ref, e1_ref, s2_ref, w2_ref, u_ref, vt_ref, x1_ref, mod_ref, fg_ref,
                       o_ref, acc_ref, p0_ref, p1_ref, c0_ref, c1_ref,
                       t1s_ref, e1s_ref, s2s_ref, w2s_ref, h2ts_ref, *, final, nb):
    g = pl.program_id(0)
    tt = h2t_ref.shape[1]
    eb = u_ref.shape[0]
    K = PEER_KEYS
    LG = 128

    @pl.when(g == 0)
    def _():
        acc_ref[...] = jnp.zeros_like(acc_ref)
        p1_ref[...] = jnp.zeros_like(p1_ref)
        c0_ref[...] = jnp.zeros_like(c0_ref)

    t1s_ref[...] = t1_ref[...]
    e1s_ref[...] = e1_ref[...]

    @pl.when(g % nb == 0)
    def _():
        h2ts_ref[...] = h2t_ref[...]

    @pl.when(jnp.logical_or(g == 0, (g - 1) % nb == 0))
    def _():
        s2s_ref[...] = s2_ref[...]
        w2s_ref[...] = w2_ref[...]

    def stages(p_new, p_old, c_new, c_old):
        nrow = eb // K
        dc = acc_ref.shape[0] // nrow
        QR = 32
        nq = K // QR
        units = [(li, q) for li in range(tt // LG) for q in range(nq)]
        assert len(units) == 2 * nrow
        for ui, (li, q) in enumerate(units):
            piece = ui // 2
            if ui % 2 == 0:
                rs = slice(piece * K, (piece + 1) * K)
                pre = jnp.dot(u_ref[rs, :], h2ts_ref[...], preferred_element_type=F32)
                for lj in range(tt // LG):
                    p_new[lj, rs, :] = pre[:, lj * LG:(lj + 1) * LG]
            else:
                ds = slice(piece * dc, (piece + 1) * dc)
                acc_ref[ds, :] += jnp.dot(vt_ref[ds, :], c_old[...], preferred_element_type=F32)
            ls = slice(li * LG, (li + 1) * LG)
            qs = slice(q * QR, (q + 1) * QR)
            gate = [None] * nrow
            for h in range(PEER_HEADS):
                s2 = s2s_ref[h, li, qs, :]
                w2 = w2s_ref[h, li, qs, :]
                for ri in range(nrow):
                    term = jnp.where(s2 > t1s_ref[li, ri, h:h + 1, :], w2, 0.0) * e1s_ref[li, ri, h:h + 1, :]
                    gate[ri] = term if gate[ri] is None else gate[ri] + term
            for ri in range(nrow):
                p = p_old[li, ri * K + q * QR:ri * K + (q + 1) * QR, :]
                coef = gate[ri] * (p * (1.0 + lax.erf(p * (1.0 / math.sqrt(2.0)))))
                c_new[ri * K + q * QR:ri * K + (q + 1) * QR, ls] = coef.astype(BF16)

    @pl.when(g % 2 == 0)
    def _():
        stages(p0_ref, p1_ref, c1_ref, c0_ref)

    @pl.when(g % 2 == 1)
    def _():
        stages(p1_ref, p0_ref, c0_ref, c1_ref)

    @pl.when(jnp.logical_and(g >= 2, (g - 2) % nb == nb - 1))
    def _():
        x2 = x1_ref[...] + mod_ref[0, 5:6, :] * acc_ref[...].T
        if final:
            x2 = _rms(x2, fg_ref[...])
        o_ref[...] = x2
        acc_ref[...] = jnp.zeros_like(acc_ref)


def _peer_dense_call(h2t, t1, e1, s2, w2, u, vt, x1, mod, fg, tt, eb, tokens_per_batch, final):
    D, N = h2t.shape
    H, _, K, LG = s2.shape
    E = u.shape[0]
    nb = E // eb
    total = (N // tt) * nb
    tpb = tokens_per_batch // tt
    rows = eb // K
    blk = lambda g, lag: jnp.clip(g - lag, 0, total - 1)
    row_spec = pl.BlockSpec((tt // LG, rows, H, LG), lambda g: (blk(g, 1) // nb, blk(g, 1) % nb, 0, 0))
    key_spec = pl.BlockSpec((H, tt // LG, K, LG), lambda g: (0, blk(g, 1) // nb, 0, 0))
    return pl.pallas_call(
        functools.partial(_peer_dense_kernel, final=final, nb=nb),
        grid=(total + 2,),
        in_specs=[
            pl.BlockSpec((D, tt), lambda g: (0, blk(g, 0) // nb)),
            row_spec, row_spec, key_spec, key_spec,
            pl.BlockSpec((eb, D), lambda g: (blk(g, 0) % nb, 0)),
            pl.BlockSpec((D, eb), lambda g: (0, blk(g, 2) % nb)),
            pl.BlockSpec((tt, D), lambda g: (blk(g, 2) // nb, 0)),
            pl.BlockSpec((1, N_MOD, D), lambda g: (blk(g, 2) // nb // tpb, 0, 0)),
            pl.BlockSpec((1, D), lambda g: (0, 0)),
        ],
        out_specs=pl.BlockSpec((tt, D), lambda g: (blk(g, 2) // nb, 0)),
        out_shape=jax.ShapeDtypeStruct((N, D), F32),
        scratch_shapes=[pltpu.VMEM((D, tt), F32),
                        pltpu.VMEM((tt // LG, eb, LG), F32), pltpu.VMEM((tt // LG, eb, LG), F32),
                        pltpu.VMEM((eb, tt), BF16), pltpu.VMEM((eb, tt), BF16),
                        pltpu.VMEM((tt // LG, rows, H, LG), F32), pltpu.VMEM((tt // LG, rows, H, LG), F32),
                        pltpu.VMEM((H, tt // LG, K, LG), F32), pltpu.VMEM((H, tt // LG, K, LG), F32),
                        pltpu.VMEM((D, tt), BF16)],
        compiler_params=_params("arbitrary"),
        name="peer_dense",
    )(h2t, t1, e1, s2, w2, u, vt, x1, mod, fg)


def _dft_tables(n):
    k = np.arange(n, dtype=np.int64)
    ang = 2.0 * np.pi * ((k[:, None] * k[None, :]) % n) / n
    return np.cos(ang), np.sin(ang)


def _seq_dft_mats(S, scale):
    R = 1 << (int(math.log2(S)) // 2)
    k = jnp.arange(S, dtype=jnp.int32)[:, None]
    w = 2.0 * math.pi / S
    ang_a = ((k * (jnp.arange(S // R, dtype=jnp.int32)[None, :] * R)) % S).astype(F32) * w
    ang_b = ((k * jnp.arange(R, dtype=jnp.int32)[None, :]) % S).astype(F32) * w
    ca, sa = jnp.cos(ang_a)[:, :, None], jnp.sin(ang_a)[:, :, None]
    cb, sb = jnp.cos(ang_b)[:, None, :], jnp.sin(ang_b)[:, None, :]
    cos_w = (ca * cb - sa * sb).reshape(S, S)
    sin_w = (sa * cb + ca * sb).reshape(S, S)
    return (cos_w * scale).astype(BF16), (sin_w * (-scale)).astype(BF16)


def _block_diag(blocks):
    G, a, b = blocks.shape
    eye = jnp.eye(G, dtype=blocks.dtype)
    return (eye[:, None, :, None] * blocks[:, :, None, :]).reshape(G * a, G * b)


def _key_head_rows(keys):
    H, K, d = keys.shape
    eye = jnp.eye(H, dtype=keys.dtype)
    return (jnp.transpose(keys, (1, 0, 2))[:, :, None, :] * eye[None, :, :, None]).reshape(K * H, H * d).astype(BF16)


def kernel(x, c, mod_w, mod_b, norm1_g, w_in, gmlp_ws, gmlp_bs, fnet_w, fnet_b, gain_a, gain_b, w_out,
           norm2_g, peer_wq, peer_k1, peer_k2, peer_u, peer_v, final_g):
    B, S, D = x.shape
    L = mod_w.shape[0]
    wa = gain_a.shape[1]
    wf = gain_b.shape[1]
    hd = wa // GMLP_HEADS
    gd = wf // FNET_GROUPS
    N = B * S
    ts = min(512, S)
    tt = min(512, S)
    eb = 8 * PEER_KEYS

    mod = _mod_call(c, mod_w, mod_b).reshape(L, B, N_MOD, D)

    avg = _block_diag(jnp.full((GMLP_HEADS, hd, hd), 1.0 / hd, F32)).astype(BF16)
    hmask = _block_diag(jnp.ones((GMLP_HEADS, CHUNK, hd), F32)).astype(BF16)
    cos_d, sin_d = _dft_tables(gd)
    cd = _block_diag(jnp.asarray(np.broadcast_to(cos_d, (FNET_GROUPS, gd, gd)), F32)).astype(BF16)
    sd = _block_diag(jnp.asarray(np.broadcast_to(sin_d, (FNET_GROUPS, gd, gd)), F32)).astype(BF16)
    wc, wsn = _seq_dft_mats(S, 1.0 / math.sqrt(S * gd))

    for l in range(L):
        ws_cat = jnp.transpose(gmlp_ws[l], (1, 0, 2)).reshape(CHUNK, GMLP_HEADS * CHUNK).astype(BF16)
        bs_full = jnp.repeat(gmlp_bs[l].T, hd, axis=1)
        ya, fc, fs = _mixer_in_call(x, mod[l], norm1_g[l][None], w_in[l].astype(BF16), avg, ws_cat, bs_full,
                                    hmask, cd, sd, ts)
        fr = _seq_dft_call(wc, wsn, fc, fs, min(512, S), wf)
        x1, h2, h2t = _mixer_out_call(x, ya, fr, mod[l], _block_diag(fnet_w[l]).astype(BF16),
                                 fnet_b[l].reshape(1, wf), gain_a[l][None], gain_b[l][None],
                                 w_out[l].astype(BF16), norm2_g[l][None], ts)
        h2 = h2.reshape(N, D)
        t1, e1, s2, w2 = _peer_scores_call(h2, peer_wq[l].astype(BF16), _key_head_rows(peer_k1[l]),
                                           _key_head_rows(peer_k2[l]), peer_k2[l].astype(BF16), tt)
        x = _peer_dense_call(h2t, t1, e1, s2, w2, peer_u[l].astype(BF16), peer_v[l].T.astype(BF16),
                             x1.reshape(N, D), mod[l], final_g[None], tt, eb, S, l == L - 1).reshape(B, S, D)
    return x
```

```python
import functools
import math

import numpy as np
import jax
import jax.numpy as jnp
from jax import lax
from jax.experimental import pallas as pl
from jax.experimental.pallas import tpu as pltpu

GMLP_HEADS = 8
FNET_GROUPS = 8
CHUNK = 128
PEER_HEADS = 8
PEER_KEYS = 128
PEER_TOPK = 16
N_MOD = 6
EPS = 1e-6

VMEM_LIMIT_BYTES = 56 * 1024 * 1024

F32 = jnp.float32
BF16 = jnp.bfloat16
NEG_INF = float("-inf")


def _params(*sem):
    return pltpu.CompilerParams(dimension_semantics=sem, vmem_limit_bytes=VMEM_LIMIT_BYTES)


def _gelu(x):
    return 0.5 * x * (1.0 + lax.erf(x * (1.0 / math.sqrt(2.0))))


def _rms(x, g):
    return x * lax.rsqrt(jnp.mean(x * x, axis=-1, keepdims=True) + EPS) * g


def _mod_kernel(c_ref, w_ref, b_ref, o_ref):
    c = c_ref[...]
    sc = c * jax.nn.sigmoid(c)
    o_ref[0] = jnp.dot(sc, w_ref[0], preferred_element_type=F32) + b_ref[0]


def _mod_call(c, mod_w, mod_b):
    L, D, M = mod_w.shape
    B = c.shape[0]
    tn = 1536
    return pl.pallas_call(
        _mod_kernel,
        grid=(L, M // tn),
        in_specs=[
            pl.BlockSpec((B, D), lambda l, j: (0, 0)),
            pl.BlockSpec((1, D, tn), lambda l, j: (l, 0, j)),
            pl.BlockSpec((1, 1, tn), lambda l, j: (l, 0, j)),
        ],
        out_specs=pl.BlockSpec((1, B, tn), lambda l, j: (l, 0, j)),
        out_shape=jax.ShapeDtypeStruct((L, B, M), F32),
        compiler_params=_params("parallel", "parallel"),
        name="adaln_mod",
    )(c, mod_w, mod_b.reshape(L, 1, M))


def _mixer_in_kernel(x_ref, mod_ref, g_ref, win_ref, avg_ref, ws_ref, bs_ref, hmask_ref, cd_ref, sd_ref,
                     ya_ref, fc_ref, fs_ref):
    ts = x_ref.shape[1]
    wa = ya_ref.shape[2]
    x = x_ref[0]
    h = _rms(x, g_ref[...]) * (1.0 + mod_ref[0, 1:2, :]) + mod_ref[0, 0:1, :]
    z = jnp.dot(h.astype(BF16), win_ref[...], preferred_element_type=F32)
    u = _gelu(z[:, :wa])
    gv = _gelu(z[:, wa:2 * wa])
    zf = z[:, 2 * wa:].astype(BF16)
    mu = jnp.dot(gv.astype(BF16), avg_ref[...], preferred_element_type=F32)
    dv = gv - mu
    var = jnp.dot((dv * dv).astype(BF16), avg_ref[...], preferred_element_type=F32)
    v = (dv * lax.rsqrt(var + EPS)).astype(BF16)
    hmask = hmask_ref[...]
    for ci in range(ts // CHUNK):
        vc = v[ci * CHUNK:(ci + 1) * CHUNK, :]
        vstack = jnp.concatenate([vc] * GMLP_HEADS, axis=0) * hmask
        sv = jnp.dot(ws_ref[...], vstack, preferred_element_type=F32) + bs_ref[...]
        ya_ref[0, ci * CHUNK:(ci + 1) * CHUNK, :] = u[ci * CHUNK:(ci + 1) * CHUNK, :] * sv
    fc_ref[...] = jnp.dot(zf, cd_ref[...], preferred_element_type=F32).astype(BF16)
    fs_ref[...] = jnp.dot(zf, sd_ref[...], preferred_element_type=F32).astype(BF16)


def _mixer_in_call(x, mod, g, win, avg, ws_cat, bs_full, hmask, cd, sd, ts):
    B, S, D = x.shape
    wa = avg.shape[0]
    wf = cd.shape[0]
    const = lambda shape: pl.BlockSpec(shape, lambda b, s: (0,) * len(shape))
    return pl.pallas_call(
        _mixer_in_kernel,
        grid=(B, S // ts),
        in_specs=[
            pl.BlockSpec((1, ts, D), lambda b, s: (b, s, 0)),
            pl.BlockSpec((1, N_MOD, D), lambda b, s: (b, 0, 0)),
            const(g.shape), const(win.shape), const(avg.shape), const(ws_cat.shape),
            const(bs_full.shape), const(hmask.shape), const(cd.shape), const(sd.shape),
        ],
        out_specs=[
            pl.BlockSpec((1, ts, wa), lambda b, s: (b, s, 0)),
            pl.BlockSpec((ts, wf), lambda b, s: (s, b)),
            pl.BlockSpec((ts, wf), lambda b, s: (s, b)),
        ],
        out_shape=[
            jax.ShapeDtypeStruct((B, S, wa), F32),
            jax.ShapeDtypeStruct((S, B * wf), BF16),
            jax.ShapeDtypeStruct((S, B * wf), BF16),
        ],
        compiler_params=_params("parallel", "parallel"),
        name="mixer_in",
    )(x, mod, g, win, avg, ws_cat, bs_full, hmask, cd, sd)


def _seq_dft_kernel(wc_ref, wsn_ref, fc_ref, fs_ref, o_ref):
    acc = jnp.dot(wc_ref[...], fc_ref[...], preferred_element_type=F32)
    acc += jnp.dot(wsn_ref[...], fs_ref[...], preferred_element_type=F32)
    o_ref[...] = acc.astype(o_ref.dtype)


def _seq_dft_call(wc, wsn, fc, fs, tm, tn):
    S = wc.shape[0]
    ncol = fc.shape[1]
    return pl.pallas_call(
        _seq_dft_kernel,
        grid=(S // tm, ncol // tn),
        in_specs=[
            pl.BlockSpec((tm, S), lambda i, j: (i, 0)),
            pl.BlockSpec((tm, S), lambda i, j: (i, 0)),
            pl.BlockSpec((S, tn), lambda i, j: (0, j)),
            pl.BlockSpec((S, tn), lambda i, j: (0, j)),
        ],
        out_specs=pl.BlockSpec((tm, tn), lambda i, j: (i, j)),
        out_shape=jax.ShapeDtypeStruct((S, ncol), BF16),
        compiler_params=_params("parallel", "parallel"),
        name="seq_dft",
    )(wc, wsn, fc, fs)


def _mixer_out_kernel(x_ref, ya_ref, fr_ref, mod_ref, fw_ref, fb_ref, ga_ref, gb_ref, wout_ref, g2_ref,
                      x1_ref, h2_ref, h2t_ref):
    wa = ya_ref.shape[2]
    yb = jnp.dot(fr_ref[...], fw_ref[...], preferred_element_type=F32) + fb_ref[...]
    na = _rms(ya_ref[0], ga_ref[...]).astype(BF16)
    nb = _rms(yb, gb_ref[...]).astype(BF16)
    mix = jnp.dot(na, wout_ref[:wa, :], preferred_element_type=F32)
    mix += jnp.dot(nb, wout_ref[wa:, :], preferred_element_type=F32)
    x1 = x_ref[0] + mod_ref[0, 2:3, :] * mix
    x1_ref[0] = x1
    h2 = _rms(x1, g2_ref[...]) * (1.0 + mod_ref[0, 4:5, :]) + mod_ref[0, 3:4, :]
    h2_ref[0] = h2.astype(BF16)
    h2t_ref[...] = h2.T.astype(BF16)


def _mixer_out_call(x, ya, fr, mod, fw_bd, fb, ga, gb, wout, g2, ts):
    B, S, D = x.shape
    wa = ya.shape[2]
    wf = fw_bd.shape[0]
    const = lambda shape: pl.BlockSpec(shape, lambda b, s: (0,) * len(shape))
    return pl.pallas_call(
        _mixer_out_kernel,
        grid=(B, S // ts),
        in_specs=[
            pl.BlockSpec((1, ts, D), lambda b, s: (b, s, 0)),
            pl.BlockSpec((1, ts, wa), lambda b, s: (b, s, 0)),
            pl.BlockSpec((ts, wf), lambda b, s: (s, b)),
            pl.BlockSpec((1, N_MOD, D), lambda b, s: (b, 0, 0)),
            const(fw_bd.shape), const(fb.shape), const(ga.shape), const(gb.shape),
            const(wout.shape), const(g2.shape),
        ],
        out_specs=[
            pl.BlockSpec((1, ts, D), lambda b, s: (b, s, 0)),
            pl.BlockSpec((1, ts, D), lambda b, s: (b, s, 0)),
            pl.BlockSpec((D, ts), lambda b, s: (0, b * (S // ts) + s)),
        ],
        out_shape=[
            jax.ShapeDtypeStruct((B, S, D), F32),
            jax.ShapeDtypeStruct((B, S, D), BF16),
            jax.ShapeDtypeStruct((D, B * S), BF16),
        ],
        compiler_params=_params("parallel", "parallel"),
        name="mixer_out",
    )(x, ya, fr, mod, fw_bd, fb, ga, gb, wout, g2)


def _sort16_pairs():
    pairs = []

    def merge(lo, n, r):
        m = 2 * r
        if m < n:
            merge(lo, n, m)
            merge(lo + r, n, m)
            pairs.extend((i, i + r) for i in range(lo + r, lo + n - r, m))
        else:
            pairs.append((lo, lo + r))

    def sort(lo, n):
        if n > 1:
            sort(lo, n // 2)
            sort(lo + n // 2, n // 2)
            merge(lo, n, 1)

    sort(0, 16)
    return pairs


_SORT16 = _sort16_pairs()
_BITONIC16 = [(i, i + s) for s in (8, 4, 2, 1) for i in range(16) if (i // s) % 2 == 0]


def _exchange(v, pairs):
    v = list(v)
    for i, j in pairs:
        v[i], v[j] = jnp.maximum(v[i], v[j]), jnp.minimum(v[i], v[j])
    return v


def _top16(vals):
    groups = [(_exchange(vals[g:g + 16], _SORT16), None) for g in range(0, len(vals), 16)]
    while len(groups) > 1:
        merged = []
        for (a, da), (b, db) in zip(groups[0::2], groups[1::2]):
            keep = [jnp.maximum(a[i], b[15 - i]) for i in range(16)]
            drop = [jnp.minimum(a[i], b[15 - i]) for i in range(16)] + [d for d in (da, db) if d is not None]
            while len(drop) > 1:
                drop = [jnp.maximum(x, y) for x, y in zip(drop[0::2], drop[1::2])] + drop[len(drop) & ~1:]
            merged.append((_exchange(keep, _BITONIC16), drop[0]))
        groups = merged
    return groups[0]


def _peer_scores_kernel(h2_ref, wq_ref, a1_ref, a2_ref, k2_ref, t1_ref, e1_ref, s2_ref, w2_ref,
                        s1_scr, s2_scr, s2h_scr):
    tt = h2_ref.shape[0]
    H, K, half = k2_ref.shape
    LG = 128
    nl = tt // LG
    nt = (((1,), (1,)), ((), ()))
    q = jnp.dot(h2_ref[...], wq_ref[...], preferred_element_type=F32).astype(BF16)
    q1 = jnp.concatenate([q[:, (2 * h) * half:(2 * h + 1) * half] for h in range(H)], axis=1)
    q2 = jnp.concatenate([q[:, (2 * h + 1) * half:(2 * h + 2) * half] for h in range(H)], axis=1)
    s1 = lax.dot_general(a1_ref[...], q1, nt, preferred_element_type=F32)
    s2 = lax.dot_general(a2_ref[...], q2, nt, preferred_element_type=F32)
    for li in range(nl):
        s1_scr[li] = s1[:, li * LG:(li + 1) * LG]
        s2_scr[li] = s2[:, li * LG:(li + 1) * LG]
    for h in range(H):
        sh = lax.dot_general(k2_ref[h], q[:, (2 * h + 1) * half:(2 * h + 2) * half], nt,
                             preferred_element_type=F32)
        for li in range(nl):
            s2h_scr[h, li] = sh[:, li * LG:(li + 1) * LG]

    def lane_group(li, carry):
        v1 = [s1_scr[li, k * H:(k + 1) * H, :] for k in range(K)]
        v2 = [s2_scr[li, k * H:(k + 1) * H, :] for k in range(K)]
        r1, d1 = _top16(v1)
        r2, d2 = _top16(v2)
        r1, r2 = r1 + [d1], r2 + [d2]
        cand = [r1[j] + r2[l] for j in range(17) for l in range(17) if (j + 1) * (l + 1) <= 17]
        pad = [jnp.full((H, LG), NEG_INF, F32)] * (-len(cand) % 16)
        top, c17 = _top16(cand + pad)
        mid = 0.5 * (top[PEER_TOPK - 1] + c17)
        m = r1[0] + r2[0]
        z = None
        for c in cand:
            term = jnp.where(c > mid, jnp.exp(c - m), 0.0)
            z = term if z is None else z + term
        scale = 0.5 / z
        for k in range(K):
            t1_ref[li, k] = mid - v1[k]
            e1_ref[li, k] = jnp.exp(v1[k] - r1[0]) * scale
        for h in range(H):
            sh = s2h_scr[h, li]
            s2_ref[h, li] = sh
            w2_ref[h, li] = jnp.exp(sh - r2[0][h:h + 1, :])
        return carry

    lax.fori_loop(0, nl, lane_group, 0)


def _peer_scores_call(h2, wq, a1, a2, k2, tt):
    N, D = h2.shape
    H, K, half = k2.shape
    LG = 128
    nl = tt // LG
    const = lambda shape: pl.BlockSpec(shape, lambda i: (0,) * len(shape))
    row_spec = pl.BlockSpec((nl, K, H, LG), lambda i: (i, 0, 0, 0))
    row_shape = jax.ShapeDtypeStruct((N // LG, K, H, LG), F32)
    key_spec = pl.BlockSpec((H, nl, K, LG), lambda i: (0, i, 0, 0))
    key_shape = jax.ShapeDtypeStruct((H, N // LG, K, LG), F32)
    return pl.pallas_call(
        _peer_scores_kernel,
        grid=(N // tt,),
        in_specs=[pl.BlockSpec((tt, D), lambda i: (i, 0)), const(wq.shape), const(a1.shape), const(a2.shape),
                  const(k2.shape)],
        out_specs=[row_spec, row_spec, key_spec, key_spec],
        out_shape=[row_shape, row_shape, key_shape, key_shape],
        scratch_shapes=[pltpu.VMEM((nl, K * H, LG), F32), pltpu.VMEM((nl, K * H, LG), F32),
                        pltpu.VMEM((H, nl, K, LG), F32)],
        compiler_params=_params("parallel"),
        name="peer_scores",
    )(h2, wq, a1, a2, k2)


def _peer_dense_kernel(h2t_ref, t1_ref, e1_ref, s2_ref, w2_ref, u_ref, vt_ref, x1_ref, mod_ref, fg_ref,
                       o_ref, acc_ref, p_ref, c_ref, t1s_ref, e1s_ref, s2s_ref, w2s_ref, *, final):
    j = pl.program_id(1)
    tt = h2t_ref.shape[1]
    eb = u_ref.shape[0]
    K = PEER_KEYS
    LG = 128
    QR = 16
    nrow = eb // K

    @pl.when(j == 0)
    def _():
        acc_ref[...] = jnp.zeros_like(acc_ref)
        s2s_ref[...] = s2_ref[...]
        w2s_ref[...] = w2_ref[...]

    t1s_ref[...] = t1_ref[...]
    e1s_ref[...] = e1_ref[...]

    pre = jnp.dot(u_ref[...], h2t_ref[...], preferred_element_type=F32)
    for li in range(tt // LG):
        p_ref[li] = pre[:, li * LG:(li + 1) * LG]
    for li in range(tt // LG):
        ls = slice(li * LG, (li + 1) * LG)
        for q in range(K // QR):
            qs = slice(q * QR, (q + 1) * QR)
            gate = [None] * nrow
            for h in range(PEER_HEADS):
                s2 = s2s_ref[h, li, qs, :]
                w2 = w2s_ref[h, li, qs, :]
                for ri in range(nrow):
                    term = jnp.where(s2 > t1s_ref[li, ri, h:h + 1, :], w2, 0.0) * e1s_ref[li, ri, h:h + 1, :]
                    gate[ri] = term if gate[ri] is None else gate[ri] + term
            for ri in range(nrow):
                rows = slice(ri * K + q * QR, ri * K + (q + 1) * QR)
                p = p_ref[li, rows, :]
                coef = gate[ri] * (p * (1.0 + lax.erf(p * (1.0 / math.sqrt(2.0)))))
                c_ref[rows, ls] = coef.astype(BF16)
    acc_ref[...] += jnp.dot(vt_ref[...], c_ref[...], preferred_element_type=F32)

    @pl.when(j == pl.num_programs(1) - 1)
    def _():
        x2 = x1_ref[...] + mod_ref[0, 5:6, :] * acc_ref[...].T
        if final:
            x2 = _rms(x2, fg_ref[...])
        o_ref[...] = x2


def _peer_dense_call(h2t, t1, e1, s2, w2, u, vt, x1, mod, fg, tt, eb, tokens_per_batch, final):
    D, N = h2t.shape
    H, _, K, LG = s2.shape
    E = u.shape[0]
    tpb = tokens_per_batch // tt
    rows = eb // K
    nl = tt // LG
    row_spec = pl.BlockSpec((nl, rows, H, LG), lambda i, j: (i, j, 0, 0))
    key_spec = pl.BlockSpec((H, nl, K, LG), lambda i, j: (0, i, 0, 0))
    return pl.pallas_call(
        functools.partial(_peer_dense_kernel, final=final),
        grid=(N // tt, E // eb),
        in_specs=[
            pl.BlockSpec((D, tt), lambda i, j: (0, i)),
            row_spec, row_spec, key_spec, key_spec,
            pl.BlockSpec((eb, D), lambda i, j: (j, 0)),
            pl.BlockSpec((D, eb), lambda i, j: (0, j)),
            pl.BlockSpec((tt, D), lambda i, j: (i, 0)),
            pl.BlockSpec((1, N_MOD, D), lambda i, j: (i // tpb, 0, 0)),
            pl.BlockSpec((1, D), lambda i, j: (0, 0)),
        ],
        out_specs=pl.BlockSpec((tt, D), lambda i, j: (i, 0)),
        out_shape=jax.ShapeDtypeStruct((N, D), F32),
        scratch_shapes=[pltpu.VMEM((D, tt), F32), pltpu.VMEM((nl, eb, LG), F32), pltpu.VMEM((eb, tt), BF16),
                        pltpu.VMEM((nl, rows, H, LG), F32), pltpu.VMEM((nl, rows, H, LG), F32),
                        pltpu.VMEM((H, nl, K, LG), F32), pltpu.VMEM((H, nl, K, LG), F32)],
        compiler_params=_params("parallel", "arbitrary"),
        name="peer_dense",
    )(h2t, t1, e1, s2, w2, u, vt, x1, mod, fg)


def _dft_tables(n):
    k = np.arange(n, dtype=np.int64)
    ang = 2.0 * np.pi * ((k[:, None] * k[None, :]) % n) / n
    return np.cos(ang), np.sin(ang)


def _seq_dft_mats(S, scale):
    R = 1 << (int(math.log2(S)) // 2)
    k = jnp.arange(S, dtype=jnp.int32)[:, None]
    w = 2.0 * math.pi / S
    ang_a = ((k * (jnp.arange(S // R, dtype=jnp.int32)[None, :] * R)) % S).astype(F32) * w
    ang_b = ((k * jnp.arange(R, dtype=jnp.int32)[None, :]) % S).astype(F32) * w
    ca, sa = jnp.cos(ang_a)[:, :, None], jnp.sin(ang_a)[:, :, None]
    cb, sb = jnp.cos(ang_b)[:, None, :], jnp.sin(ang_b)[:, None, :]
    cos_w = (ca * cb - sa * sb).reshape(S, S)
    sin_w = (sa * cb + ca * sb).reshape(S, S)
    return (cos_w * scale).astype(BF16), (sin_w * (-scale)).astype(BF16)


def _block_diag(blocks):
    G, a, b = blocks.shape
    eye = jnp.eye(G, dtype=blocks.dtype)
    return (eye[:, None, :, None] * blocks[:, :, None, :]).reshape(G * a, G * b)


def _key_head_rows(keys):
    H, K, d = keys.shape
    eye = jnp.eye(H, dtype=keys.dtype)
    return (jnp.transpose(keys, (1, 0, 2))[:, :, None, :] * eye[None, :, :, None]).reshape(K * H, H * d).astype(BF16)


def kernel(x, c, mod_w, mod_b, norm1_g, w_in, gmlp_ws, gmlp_bs, fnet_w, fnet_b, gain_a, gain_b, w_out,
           norm2_g, peer_wq, peer_k1, peer_k2, peer_u, peer_v, final_g):
    B, S, D = x.shape
    L = mod_w.shape[0]
    wa = gain_a.shape[1]
    wf = gain_b.shape[1]
    hd = wa // GMLP_HEADS
    gd = wf // FNET_GROUPS
    N = B * S
    ts = min(512, S)
    tt = min(512, S)
    eb = 16 * PEER_KEYS

    mod = _mod_call(c, mod_w, mod_b).reshape(L, B, N_MOD, D)

    avg = _block_diag(jnp.full((GMLP_HEADS, hd, hd), 1.0 / hd, F32)).astype(BF16)
    hmask = _block_diag(jnp.ones((GMLP_HEADS, CHUNK, hd), F32)).astype(BF16)
    cos_d, sin_d = _dft_tables(gd)
    cd = _block_diag(jnp.asarray(np.broadcast_to(cos_d, (FNET_GROUPS, gd, gd)), F32)).astype(BF16)
    sd = _block_diag(jnp.asarray(np.broadcast_to(sin_d, (FNET_GROUPS, gd, gd)), F32)).astype(BF16)
    wc, wsn = _seq_dft_mats(S, 1.0 / math.sqrt(S * gd))

    for l in range(L):
        ws_cat = jnp.transpose(gmlp_ws[l], (1, 0, 2)).reshape(CHUNK, GMLP_HEADS * CHUNK).astype(BF16)
        bs_full = jnp.repeat(gmlp_bs[l].T, hd, axis=1)
        ya, fc, fs = _mixer_in_call(x, mod[l], norm1_g[l][None], w_in[l].astype(BF16), avg, ws_cat, bs_full,
                                    hmask, cd, sd, ts)
        fr = _seq_dft_call(wc, wsn, fc, fs, min(512, S), wf)
        x1, h2, h2t = _mixer_out_call(x, ya, fr, mod[l], _block_diag(fnet_w[l]).astype(BF16),
                                 fnet_b[l].reshape(1, wf), gain_a[l][None], gain_b[l][None],
                                 w_out[l].astype(BF16), norm2_g[l][None], ts)
        h2 = h2.reshape(N, D)
        t1, e1, s2, w2 = _peer_scores_call(h2, peer_wq[l].astype(BF16), _key_head_rows(peer_k1[l]),
                                           _key_head_rows(peer_k2[l]), peer_k2[l].astype(BF16), tt)
        x = _peer_dense_call(h2t, t1, e1, s2, w2, peer_u[l].astype(BF16), peer_v[l].T.astype(BF16),
                             x1.reshape(N, D), mod[l], final_g[None], tt, eb, S, l == L - 1).reshape(B, S, D)
    return x
```

```python
import functools
import math

import numpy as np
import jax
import jax.numpy as jnp
from jax import lax
from jax.experimental import pallas as pl
from jax.experimental.pallas import tpu as pltpu

GMLP_HEADS = 8
FNET_GROUPS = 8
CHUNK = 128
PEER_HEADS = 8
PEER_KEYS = 128
PEER_TOPK = 16
N_MOD = 6
EPS = 1e-6

LANES = 128
VMEM_LIMIT_BYTES = 56 * 1024 * 1024
TOKEN_TILE = 512
MOD_COLS = 1536

F32 = jnp.float32
BF16 = jnp.bfloat16
NEG_INF = float("-inf")


def _params(*sem):
    return pltpu.CompilerParams(dimension_semantics=sem, vmem_limit_bytes=VMEM_LIMIT_BYTES)


def _gelu(x):
    return 0.5 * x * (1.0 + lax.erf(x * (1.0 / math.sqrt(2.0))))


def _rms(x, g):
    return x * lax.rsqrt(jnp.mean(x * x, axis=-1, keepdims=True) + EPS) * g


def _mod_kernel(c_ref, w_ref, b_ref, o_ref):
    c = c_ref[...]
    sc = c * jax.nn.sigmoid(c)
    o_ref[0] = jnp.dot(sc, w_ref[0], preferred_element_type=F32) + b_ref[0]


def _mod_call(c, mod_w, mod_b):
    L, D, M = mod_w.shape
    B = c.shape[0]
    tn = MOD_COLS
    return pl.pallas_call(
        _mod_kernel,
        grid=(L, M // tn),
        in_specs=[
            pl.BlockSpec((B, D), lambda l, j: (0, 0)),
            pl.BlockSpec((1, D, tn), lambda l, j: (l, 0, j)),
            pl.BlockSpec((1, 1, tn), lambda l, j: (l, 0, j)),
        ],
        out_specs=pl.BlockSpec((1, B, tn), lambda l, j: (l, 0, j)),
        out_shape=jax.ShapeDtypeStruct((L, B, M), F32),
        compiler_params=_params("parallel", "parallel"),
        name="adaln_mod",
    )(c, mod_w, mod_b.reshape(L, 1, M))


def _mixer_in_kernel(x_ref, mod_ref, g_ref, win_ref, avg_ref, ws_ref, bs_ref, hmask_ref, cd_ref, sd_ref,
                     ya_ref, fc_ref, fs_ref):
    ts = x_ref.shape[1]
    wa = ya_ref.shape[2]
    x = x_ref[0]
    h = _rms(x, g_ref[...]) * (1.0 + mod_ref[0, 1:2, :]) + mod_ref[0, 0:1, :]
    z = jnp.dot(h.astype(BF16), win_ref[...], preferred_element_type=F32)
    u = _gelu(z[:, :wa])
    gv = _gelu(z[:, wa:2 * wa])
    zf = z[:, 2 * wa:].astype(BF16)
    mu = jnp.dot(gv.astype(BF16), avg_ref[...], preferred_element_type=F32)
    dv = gv - mu
    var = jnp.dot((dv * dv).astype(BF16), avg_ref[...], preferred_element_type=F32)
    v = (dv * lax.rsqrt(var + EPS)).astype(BF16)
    hmask = hmask_ref[...]
    for ci in range(ts // CHUNK):
        vc = v[ci * CHUNK:(ci + 1) * CHUNK, :]
        vstack = jnp.concatenate([vc] * GMLP_HEADS, axis=0) * hmask
        sv = jnp.dot(ws_ref[...], vstack, preferred_element_type=F32) + bs_ref[...]
        ya_ref[0, ci * CHUNK:(ci + 1) * CHUNK, :] = u[ci * CHUNK:(ci + 1) * CHUNK, :] * sv
    fc_ref[...] = jnp.dot(zf, cd_ref[...], preferred_element_type=F32).astype(BF16)
    fs_ref[...] = jnp.dot(zf, sd_ref[...], preferred_element_type=F32).astype(BF16)


def _mixer_in_call(x, mod, g, win, avg, ws_cat, bs_full, hmask, cd, sd, ts):
    B, S, D = x.shape
    wa = avg.shape[0]
    wf = cd.shape[0]
    const = lambda shape: pl.BlockSpec(shape, lambda b, s: (0,) * len(shape))
    return pl.pallas_call(
        _mixer_in_kernel,
        grid=(B, S // ts),
        in_specs=[
            pl.BlockSpec((1, ts, D), lambda b, s: (b, s, 0)),
            pl.BlockSpec((1, N_MOD, D), lambda b, s: (b, 0, 0)),
            const(g.shape), const(win.shape), const(avg.shape), const(ws_cat.shape),
            const(bs_full.shape), const(hmask.shape), const(cd.shape), const(sd.shape),
        ],
        out_specs=[
            pl.BlockSpec((1, ts, wa), lambda b, s: (b, s, 0)),
            pl.BlockSpec((ts, wf), lambda b, s: (s, b)),
            pl.BlockSpec((ts, wf), lambda b, s: (s, b)),
        ],
        out_shape=[
            jax.ShapeDtypeStruct((B, S, wa), F32),
            jax.ShapeDtypeStruct((S, B * wf), BF16),
            jax.ShapeDtypeStruct((S, B * wf), BF16),
        ],
        compiler_params=_params("parallel", "parallel"),
        name="mixer_in",
    )(x, mod, g, win, avg, ws_cat, bs_full, hmask, cd, sd)


def _seq_dft_kernel(wc_ref, wsn_ref, fc_ref, fs_ref, o_ref):
    acc = jnp.dot(wc_ref[...], fc_ref[...], preferred_element_type=F32)
    acc += jnp.dot(wsn_ref[...], fs_ref[...], preferred_element_type=F32)
    o_ref[...] = acc.astype(o_ref.dtype)


def _seq_dft_call(wc, wsn, fc, fs, tm, tn):
    S = wc.shape[0]
    ncol = fc.shape[1]
    return pl.pallas_call(
        _seq_dft_kernel,
        grid=(S // tm, ncol // tn),
        in_specs=[
            pl.BlockSpec((tm, S), lambda i, j: (i, 0)),
            pl.BlockSpec((tm, S), lambda i, j: (i, 0)),
            pl.BlockSpec((S, tn), lambda i, j: (0, j)),
            pl.BlockSpec((S, tn), lambda i, j: (0, j)),
        ],
        out_specs=pl.BlockSpec((tm, tn), lambda i, j: (i, j)),
        out_shape=jax.ShapeDtypeStruct((S, ncol), BF16),
        compiler_params=_params("parallel", "parallel"),
        name="seq_dft",
    )(wc, wsn, fc, fs)


def _mixer_out_kernel(x_ref, ya_ref, fr_ref, mod_ref, fw_ref, fb_ref, ga_ref, gb_ref, wout_ref, g2_ref,
                      x1_ref, h2_ref, h2t_ref):
    wa = ya_ref.shape[2]
    yb = jnp.dot(fr_ref[...], fw_ref[...], preferred_element_type=F32) + fb_ref[...]
    na = _rms(ya_ref[0], ga_ref[...]).astype(BF16)
    nb = _rms(yb, gb_ref[...]).astype(BF16)
    mix = jnp.dot(na, wout_ref[:wa, :], preferred_element_type=F32)
    mix += jnp.dot(nb, wout_ref[wa:, :], preferred_element_type=F32)
    x1 = x_ref[0] + mod_ref[0, 2:3, :] * mix
    x1_ref[0] = x1
    h2 = _rms(x1, g2_ref[...]) * (1.0 + mod_ref[0, 4:5, :]) + mod_ref[0, 3:4, :]
    h2_ref[0] = h2.astype(BF16)
    h2t_ref[...] = h2.T.astype(BF16)


def _mixer_out_call(x, ya, fr, mod, fw_bd, fb, ga, gb, wout, g2, ts):
    B, S, D = x.shape
    wa = ya.shape[2]
    wf = fw_bd.shape[0]
    const = lambda shape: pl.BlockSpec(shape, lambda b, s: (0,) * len(shape))
    return pl.pallas_call(
        _mixer_out_kernel,
        grid=(B, S // ts),
        in_specs=[
            pl.BlockSpec((1, ts, D), lambda b, s: (b, s, 0)),
            pl.BlockSpec((1, ts, wa), lambda b, s: (b, s, 0)),
            pl.BlockSpec((ts, wf), lambda b, s: (s, b)),
            pl.BlockSpec((1, N_MOD, D), lambda b, s: (b, 0, 0)),
            const(fw_bd.shape), const(fb.shape), const(ga.shape), const(gb.shape),
            const(wout.shape), const(g2.shape),
        ],
        out_specs=[
            pl.BlockSpec((1, ts, D), lambda b, s: (b, s, 0)),
            pl.BlockSpec((1, ts, D), lambda b, s: (b, s, 0)),
            pl.BlockSpec((D, ts), lambda b, s: (0, b * (S // ts) + s)),
        ],
        out_shape=[
            jax.ShapeDtypeStruct((B, S, D), F32),
            jax.ShapeDtypeStruct((B, S, D), BF16),
            jax.ShapeDtypeStruct((D, B * S), BF16),
        ],
        compiler_params=_params("parallel", "parallel"),
        name="mixer_out",
    )(x, ya, fr, mod, fw_bd, fb, ga, gb, wout, g2)


def _sort16_pairs():
    pairs = []

    def merge(lo, n, r):
        m = 2 * r
        if m < n:
            merge(lo, n, m)
            merge(lo + r, n, m)
            pairs.extend((i, i + r) for i in range(lo + r, lo + n - r, m))
        else:
            pairs.append((lo, lo + r))

    def sort(lo, n):
        if n > 1:
            sort(lo, n // 2)
            sort(lo + n // 2, n // 2)
            merge(lo, n, 1)

    sort(0, 16)
    return pairs


_SORT16 = _sort16_pairs()
_BITONIC16 = [(i, i + s) for s in (8, 4, 2, 1) for i in range(16) if (i // s) % 2 == 0]


def _exchange(v, pairs):
    v = list(v)
    for i, j in pairs:
        v[i], v[j] = jnp.maximum(v[i], v[j]), jnp.minimum(v[i], v[j])
    return v


def _top16(vals):
    groups = [(_exchange(vals[g:g + 16], _SORT16), None) for g in range(0, len(vals), 16)]
    while len(groups) > 1:
        merged = []
        for (a, da), (b, db) in zip(groups[0::2], groups[1::2]):
            keep = [jnp.maximum(a[i], b[15 - i]) for i in range(16)]
            drop = [jnp.minimum(a[i], b[15 - i]) for i in range(16)] + [d for d in (da, db) if d is not None]
            while len(drop) > 1:
                drop = [jnp.maximum(x, y) for x, y in zip(drop[0::2], drop[1::2])] + drop[len(drop) & ~1:]
            merged.append((_exchange(keep, _BITONIC16), drop[0]))
        groups = merged
    return groups[0]


def _peer_scores_kernel(h2_ref, wq_ref, a1_ref, a2_ref, k2_ref, t1_ref, e1_ref, s2_ref, w2_ref,
                        s1_scr, s2_scr, s2h_scr):
    tt = h2_ref.shape[0]
    H, K, half = k2_ref.shape
    LG = LANES
    nl = tt // LG
    nt = (((1,), (1,)), ((), ()))
    q = jnp.dot(h2_ref[...], wq_ref[...], preferred_element_type=F32).astype(BF16)
    q1 = jnp.concatenate([q[:, (2 * h) * half:(2 * h + 1) * half] for h in range(H)], axis=1)
    q2 = jnp.concatenate([q[:, (2 * h + 1) * half:(2 * h + 2) * half] for h in range(H)], axis=1)
    s1 = lax.dot_general(a1_ref[...], q1, nt, preferred_element_type=F32)
    s2 = lax.dot_general(a2_ref[...], q2, nt, preferred_element_type=F32)
    for li in range(nl):
        s1_scr[li] = s1[:, li * LG:(li + 1) * LG]
        s2_scr[li] = s2[:, li * LG:(li + 1) * LG]
    for h in range(H):
        sh = lax.dot_general(k2_ref[h], q[:, (2 * h + 1) * half:(2 * h + 2) * half], nt,
                             preferred_element_type=F32)
        for li in range(nl):
            s2h_scr[h, li] = sh[:, li * LG:(li + 1) * LG]

    def lane_group(li, carry):
        v1 = [s1_scr[li, k * H:(k + 1) * H, :] for k in range(K)]
        v2 = [s2_scr[li, k * H:(k + 1) * H, :] for k in range(K)]
        r1, d1 = _top16(v1)
        r2, d2 = _top16(v2)
        r1, r2 = r1 + [d1], r2 + [d2]
        cand = [r1[j] + r2[l] for j in range(17) for l in range(17) if (j + 1) * (l + 1) <= 17]
        pad = [jnp.full((H, LG), NEG_INF, F32)] * (-len(cand) % 16)
        top, c17 = _top16(cand + pad)
        mid = 0.5 * (top[PEER_TOPK - 1] + c17)
        m = r1[0] + r2[0]
        z = None
        for c in cand:
            term = jnp.where(c > mid, jnp.exp(c - m), 0.0)
            z = term if z is None else z + term
        scale = math.sqrt(0.5) / z
        for k in range(K):
            t1_ref[li, k] = mid - v1[k]
            e1_ref[li, k] = jnp.exp(v1[k] - r1[0]) * scale
        for h in range(H):
            sh = s2h_scr[h, li]
            s2_ref[h, li] = sh
            w2_ref[h, li] = jnp.exp(sh - r2[0][h:h + 1, :])
        return carry

    lax.fori_loop(0, nl, lane_group, 0)


def _peer_scores_call(h2, wq, a1, a2, k2, tt):
    N, D = h2.shape
    H, K, half = k2.shape
    LG = LANES
    nl = tt // LG
    const = lambda shape: pl.BlockSpec(shape, lambda i: (0,) * len(shape))
    row_spec = pl.BlockSpec((nl, K, H, LG), lambda i: (i, 0, 0, 0))
    row_shape = jax.ShapeDtypeStruct((N // LG, K, H, LG), F32)
    key_spec = pl.BlockSpec((H, nl, K, LG), lambda i: (0, i, 0, 0))
    key_shape = jax.ShapeDtypeStruct((H, N // LG, K, LG), F32)
    return pl.pallas_call(
        _peer_scores_kernel,
        grid=(N // tt,),
        in_specs=[pl.BlockSpec((tt, D), lambda i: (i, 0)), const(wq.shape), const(a1.shape), const(a2.shape),
                  const(k2.shape)],
        out_specs=[row_spec, row_spec, key_spec, key_spec],
        out_shape=[row_shape, row_shape, key_shape, key_shape],
        scratch_shapes=[pltpu.VMEM((nl, K * H, LG), F32), pltpu.VMEM((nl, K * H, LG), F32),
                        pltpu.VMEM((H, nl, K, LG), F32)],
        compiler_params=_params("parallel"),
        name="peer_scores",
    )(h2, wq, a1, a2, k2)


def _peer_dense_kernel(h2t_ref, t1_ref, e1_ref, s2_ref, w2_ref, u_ref, vt_ref, x1_ref, mod_ref, fg_ref,
                       o_ref, acc_ref, p0_ref, p1_ref, c0_ref, c1_ref,
                       t1s_ref, e1s_ref, s2s_ref, w2s_ref, h2ts_ref, *, final, nb):
    g = pl.program_id(0)
    tt = h2t_ref.shape[1]
    eb = u_ref.shape[0]
    K = PEER_KEYS
    LG = LANES

    @pl.when(g == 0)
    def _():
        acc_ref[...] = jnp.zeros_like(acc_ref)
        p1_ref[...] = jnp.zeros_like(p1_ref)
        c0_ref[...] = jnp.zeros_like(c0_ref)

    t1s_ref[...] = t1_ref[...]
    e1s_ref[...] = e1_ref[...]

    @pl.when(g % nb == 0)
    def _():
        h2ts_ref[...] = h2t_ref[...]

    @pl.when(jnp.logical_or(g == 0, (g - 1) % nb == 0))
    def _():
        s2s_ref[...] = s2_ref[...]
        w2s_ref[...] = w2_ref[...]

    def stages(p_new, p_old, c_new, c_old):
        nrow = eb // K
        dc = acc_ref.shape[0] // nrow
        QR = 32
        nq = K // QR
        units = [(li, q) for li in range(tt // LG) for q in range(nq)]
        assert len(units) == 2 * nrow
        for ui, (li, q) in enumerate(units):
            piece = ui // 2
            if ui % 2 == 0:
                rs = slice(piece * K, (piece + 1) * K)
                pre = jnp.dot(u_ref[rs, :], h2ts_ref[...], preferred_element_type=F32)
                for lj in range(tt // LG):
                    p_new[lj, rs, :] = pre[:, lj * LG:(lj + 1) * LG]
            else:
                ds = slice(piece * dc, (piece + 1) * dc)
                acc_ref[ds, :] += lax.dot_general(vt_ref[:, ds], c_old[...], (((0,), (0,)), ((), ())),
                                                  preferred_element_type=F32)
            ls = slice(li * LG, (li + 1) * LG)
            qs = slice(q * QR, (q + 1) * QR)
            gate = [None] * nrow
            for h in range(PEER_HEADS):
                s2 = s2s_ref[h, li, qs, :]
                w2 = w2s_ref[h, li, qs, :]
                for ri in range(nrow):
                    term = jnp.where(s2 > t1s_ref[li, ri, h:h + 1, :], w2, 0.0) * e1s_ref[li, ri, h:h + 1, :]
                    gate[ri] = term if gate[ri] is None else gate[ri] + term
            for ri in range(nrow):
                p = p_old[li, ri * K + q * QR:ri * K + (q + 1) * QR, :]
                coef = gate[ri] * (p * (1.0 + lax.erf(p)))
                c_new[ri * K + q * QR:ri * K + (q + 1) * QR, ls] = coef.astype(BF16)

    @pl.when(g % 2 == 0)
    def _():
        stages(p0_ref, p1_ref, c1_ref, c0_ref)

    @pl.when(g % 2 == 1)
    def _():
        stages(p1_ref, p0_ref, c0_ref, c1_ref)

    @pl.when(jnp.logical_and(g >= 2, (g - 2) % nb == nb - 1))
    def _():
        x2 = x1_ref[...] + mod_ref[0, 5:6, :] * acc_ref[...].T
        if final:
            x2 = _rms(x2, fg_ref[...])
        o_ref[...] = x2
        acc_ref[...] = jnp.zeros_like(acc_ref)


def _peer_dense_call(h2t, t1, e1, s2, w2, u, vt, x1, mod, fg, tt, eb, tokens_per_batch, layer, final):
    D, N = h2t.shape
    H, _, K, LG = s2.shape
    E = u.shape[1]
    nb = E // eb
    total = (N // tt) * nb
    tpb = tokens_per_batch // tt
    rows = eb // K
    blk = lambda g, lag: jnp.clip(g - lag, 0, total - 1)
    row_spec = pl.BlockSpec((tt // LG, rows, H, LG), lambda g: (blk(g, 1) // nb, blk(g, 1) % nb, 0, 0))
    key_spec = pl.BlockSpec((H, tt // LG, K, LG), lambda g: (0, blk(g, 1) // nb, 0, 0))
    return pl.pallas_call(
        functools.partial(_peer_dense_kernel, final=final, nb=nb),
        grid=(total + 2,),
        in_specs=[
            pl.BlockSpec((D, tt), lambda g: (0, blk(g, 0) // nb)),
            row_spec, row_spec, key_spec, key_spec,
            pl.BlockSpec((None, eb, D), lambda g: (layer, blk(g, 0) % nb, 0)),
            pl.BlockSpec((None, eb, D), lambda g: (layer, blk(g, 2) % nb, 0)),
            pl.BlockSpec((tt, D), lambda g: (blk(g, 2) // nb, 0)),
            pl.BlockSpec((1, N_MOD, D), lambda g: (blk(g, 2) // nb // tpb, 0, 0)),
            pl.BlockSpec((1, D), lambda g: (0, 0)),
        ],
        out_specs=pl.BlockSpec((tt, D), lambda g: (blk(g, 2) // nb, 0)),
        out_shape=jax.ShapeDtypeStruct((N, D), F32),
        scratch_shapes=[pltpu.VMEM((D, tt), F32),
                        pltpu.VMEM((tt // LG, eb, LG), F32), pltpu.VMEM((tt // LG, eb, LG), F32),
                        pltpu.VMEM((eb, tt), BF16), pltpu.VMEM((eb, tt), BF16),
                        pltpu.VMEM((tt // LG, rows, H, LG), F32), pltpu.VMEM((tt // LG, rows, H, LG), F32),
                        pltpu.VMEM((H, tt // LG, K, LG), F32), pltpu.VMEM((H, tt // LG, K, LG), F32),
                        pltpu.VMEM((D, tt), BF16)],
        compiler_params=_params("arbitrary"),
        name="peer_dense",
    )(h2t, t1, e1, s2, w2, u, vt, x1, mod, fg)


def _dft_tables(n):
    k = np.arange(n, dtype=np.int64)
    ang = 2.0 * np.pi * ((k[:, None] * k[None, :]) % n) / n
    return np.cos(ang), np.sin(ang)


def _seq_dft_mats(S, scale):
    R = 1 << (int(math.log2(S)) // 2)
    k = jnp.arange(S, dtype=jnp.int32)[:, None]
    w = 2.0 * math.pi / S
    ang_a = ((k * (jnp.arange(S // R, dtype=jnp.int32)[None, :] * R)) % S).astype(F32) * w
    ang_b = ((k * jnp.arange(R, dtype=jnp.int32)[None, :]) % S).astype(F32) * w
    ca, sa = jnp.cos(ang_a)[:, :, None], jnp.sin(ang_a)[:, :, None]
    cb, sb = jnp.cos(ang_b)[:, None, :], jnp.sin(ang_b)[:, None, :]
    cos_w = (ca * cb - sa * sb).reshape(S, S)
    sin_w = (sa * cb + ca * sb).reshape(S, S)
    return (cos_w * scale).astype(BF16), (sin_w * (-scale)).astype(BF16)


def _block_diag(blocks):
    G, a, b = blocks.shape
    eye = jnp.eye(G, dtype=blocks.dtype)
    return (eye[:, None, :, None] * blocks[:, :, None, :]).reshape(G * a, G * b)


def _key_head_rows(keys):
    H, K, d = keys.shape
    eye = jnp.eye(H, dtype=keys.dtype)
    return (jnp.transpose(keys, (1, 0, 2))[:, :, None, :] * eye[None, :, :, None]).reshape(K * H, H * d).astype(BF16)


def kernel(x, c, mod_w, mod_b, norm1_g, w_in, gmlp_ws, gmlp_bs, fnet_w, fnet_b, gain_a, gain_b, w_out,
           norm2_g, peer_wq, peer_k1, peer_k2, peer_u, peer_v, final_g):
    B, S, D = x.shape
    L = mod_w.shape[0]
    wa = gain_a.shape[1]
    wf = gain_b.shape[1]
    hd = wa // GMLP_HEADS
    gd = wf // FNET_GROUPS
    N = B * S
    ts = min(TOKEN_TILE, S)
    tt = min(TOKEN_TILE, S)
    eb = 8 * PEER_KEYS

    mod = _mod_call(c, mod_w, mod_b).reshape(L, B, N_MOD, D)

    avg = _block_diag(jnp.full((GMLP_HEADS, hd, hd), 1.0 / hd, F32)).astype(BF16)
    hmask = _block_diag(jnp.ones((GMLP_HEADS, CHUNK, hd), F32)).astype(BF16)
    cos_d, sin_d = _dft_tables(gd)
    cd = _block_diag(jnp.asarray(np.broadcast_to(cos_d, (FNET_GROUPS, gd, gd)), F32)).astype(BF16)
    sd = _block_diag(jnp.asarray(np.broadcast_to(sin_d, (FNET_GROUPS, gd, gd)), F32)).astype(BF16)
    wc, wsn = _seq_dft_mats(S, 1.0 / math.sqrt(S * gd))
    u_all = (peer_u * math.sqrt(0.5)).astype(BF16)
    vt_all = peer_v.astype(BF16)

    for l in range(L):
        ws_cat = jnp.transpose(gmlp_ws[l], (1, 0, 2)).reshape(CHUNK, GMLP_HEADS * CHUNK).astype(BF16)
        bs_full = jnp.repeat(gmlp_bs[l].T, hd, axis=1)
        ya, fc, fs = _mixer_in_call(x, mod[l], norm1_g[l][None], w_in[l].astype(BF16), avg, ws_cat, bs_full,
                                    hmask, cd, sd, ts)
        fr = _seq_dft_call(wc, wsn, fc, fs, ts, wf)
        x1, h2, h2t = _mixer_out_call(x, ya, fr, mod[l], _block_diag(fnet_w[l]).astype(BF16),
                                 fnet_b[l].reshape(1, wf), gain_a[l][None], gain_b[l][None],
                                 w_out[l].astype(BF16), norm2_g[l][None], ts)
        h2 = h2.reshape(N, D)
        t1, e1, s2, w2 = _peer_scores_call(h2, peer_wq[l].astype(BF16), _key_head_rows(peer_k1[l]),
                                           _key_head_rows(peer_k2[l]), peer_k2[l].astype(BF16), tt)
        x = _peer_dense_call(h2t, t1, e1, s2, w2, u_all, vt_all, x1.reshape(N, D), mod[l], final_g[None],
                             tt, eb, S, l, l == L - 1).reshape(B, S, D)
    return x
```

```python
import functools
import math

import numpy as np
import jax
import jax.numpy as jnp
from jax import lax
from jax.experimental import pallas as pl
from jax.experimental.pallas import tpu as pltpu

GMLP_HEADS = 8
FNET_GROUPS = 8
CHUNK = 128
PEER_HEADS = 8
PEER_KEYS = 128
PEER_TOPK = 16
N_MOD = 6
EPS = 1e-6

LANES = 128
VMEM_LIMIT_BYTES = 56 * 1024 * 1024
TOKEN_TILE = 512
MOD_COLS = 1536

F32 = jnp.float32
BF16 = jnp.bfloat16
NEG_INF = float("-inf")


def _params(*sem):
    return pltpu.CompilerParams(dimension_semantics=sem, vmem_limit_bytes=VMEM_LIMIT_BYTES)


def _gelu(x):
    return 0.5 * x * (1.0 + lax.erf(x * (1.0 / math.sqrt(2.0))))


def _rms(x, g):
    return x * lax.rsqrt(jnp.mean(x * x, axis=-1, keepdims=True) + EPS) * g


def _mod_kernel(c_ref, w_ref, b_ref, o_ref):
    c = c_ref[...]
    sc = c * jax.nn.sigmoid(c)
    o_ref[0] = jnp.dot(sc, w_ref[0], preferred_element_type=F32) + b_ref[0]


def _mod_call(c, mod_w, mod_b):
    L, D, M = mod_w.shape
    B = c.shape[0]
    tn = MOD_COLS
    return pl.pallas_call(
        _mod_kernel,
        grid=(L, M // tn),
        in_specs=[
            pl.BlockSpec((B, D), lambda l, j: (0, 0)),
            pl.BlockSpec((1, D, tn), lambda l, j: (l, 0, j)),
            pl.BlockSpec((1, 1, tn), lambda l, j: (l, 0, j)),
        ],
        out_specs=pl.BlockSpec((1, B, tn), lambda l, j: (l, 0, j)),
        out_shape=jax.ShapeDtypeStruct((L, B, M), F32),
        compiler_params=_params("parallel", "parallel"),
        name="adaln_mod",
    )(c, mod_w, mod_b.reshape(L, 1, M))


def _mixer_in_kernel(x_ref, mod_ref, g_ref, win_ref, avg_ref, ws_ref, bs_ref, hmask_ref, cd_ref, sd_ref,
                     ya_ref, fc_ref, fs_ref):
    ts = x_ref.shape[1]
    wa = ya_ref.shape[2]
    x = x_ref[0]
    h = _rms(x, g_ref[...]) * (1.0 + mod_ref[0, 1:2, :]) + mod_ref[0, 0:1, :]
    z = jnp.dot(h.astype(BF16), win_ref[...], preferred_element_type=F32)
    u = _gelu(z[:, :wa])
    gv = _gelu(z[:, wa:2 * wa])
    zf = z[:, 2 * wa:].astype(BF16)
    mu = jnp.dot(gv.astype(BF16), avg_ref[...], preferred_element_type=F32)
    dv = gv - mu
    var = jnp.dot((dv * dv).astype(BF16), avg_ref[...], preferred_element_type=F32)
    v = (dv * lax.rsqrt(var + EPS)).astype(BF16)
    hmask = hmask_ref[...]
    for ci in range(ts // CHUNK):
        vc = v[ci * CHUNK:(ci + 1) * CHUNK, :]
        vstack = jnp.concatenate([vc] * GMLP_HEADS, axis=0) * hmask
        sv = jnp.dot(ws_ref[...], vstack, preferred_element_type=F32) + bs_ref[...]
        ya_ref[0, ci * CHUNK:(ci + 1) * CHUNK, :] = u[ci * CHUNK:(ci + 1) * CHUNK, :] * sv
    fc_ref[...] = jnp.dot(zf, cd_ref[...], preferred_element_type=F32).astype(BF16)
    fs_ref[...] = jnp.dot(zf, sd_ref[...], preferred_element_type=F32).astype(BF16)


def _mixer_in_call(x, mod, g, win, avg, ws_cat, bs_full, hmask, cd, sd, ts):
    B, S, D = x.shape
    wa = avg.shape[0]
    wf = cd.shape[0]
    const = lambda shape: pl.BlockSpec(shape, lambda b, s: (0,) * len(shape))
    return pl.pallas_call(
        _mixer_in_kernel,
        grid=(B, S // ts),
        in_specs=[
            pl.BlockSpec((1, ts, D), lambda b, s: (b, s, 0)),
            pl.BlockSpec((1, N_MOD, D), lambda b, s: (b, 0, 0)),
            const(g.shape), const(win.shape), const(avg.shape), const(ws_cat.shape),
            const(bs_full.shape), const(hmask.shape), const(cd.shape), const(sd.shape),
        ],
        out_specs=[
            pl.BlockSpec((1, ts, wa), lambda b, s: (b, s, 0)),
            pl.BlockSpec((None, ts, wf), lambda b, s: (b, s, 0)),
            pl.BlockSpec((None, ts, wf), lambda b, s: (b, s, 0)),
        ],
        out_shape=[
            jax.ShapeDtypeStruct((B, S, wa), F32),
            jax.ShapeDtypeStruct((B, S, wf), BF16),
            jax.ShapeDtypeStruct((B, S, wf), BF16),
        ],
        compiler_params=_params("parallel", "parallel"),
        name="mixer_in",
    )(x, mod, g, win, avg, ws_cat, bs_full, hmask, cd, sd)


def _seq_dft_kernel(wc_ref, wsn_ref, fc_ref, fs_ref, o_ref):
    acc = jnp.dot(wc_ref[...], fc_ref[...], preferred_element_type=F32)
    acc += jnp.dot(wsn_ref[...], fs_ref[...], preferred_element_type=F32)
    o_ref[...] = acc.astype(o_ref.dtype)


def _seq_dft_call(wc, wsn, fc, fs, tm):
    B, S, wf = fc.shape
    return pl.pallas_call(
        _seq_dft_kernel,
        grid=(S // tm, B),
        in_specs=[
            pl.BlockSpec((tm, S), lambda i, j: (i, 0)),
            pl.BlockSpec((tm, S), lambda i, j: (i, 0)),
            pl.BlockSpec((None, S, wf), lambda i, j: (j, 0, 0)),
            pl.BlockSpec((None, S, wf), lambda i, j: (j, 0, 0)),
        ],
        out_specs=pl.BlockSpec((None, tm, wf), lambda i, j: (j, i, 0)),
        out_shape=jax.ShapeDtypeStruct((B, S, wf), BF16),
        compiler_params=_params("parallel", "parallel"),
        name="seq_dft",
    )(wc, wsn, fc, fs)


def _mixer_out_kernel(x_ref, ya_ref, fr_ref, mod_ref, fw_ref, fb_ref, ga_ref, gb_ref, wout_ref, g2_ref,
                      x1_ref, h2_ref, h2t_ref):
    wa = ya_ref.shape[2]
    yb = jnp.dot(fr_ref[...], fw_ref[...], preferred_element_type=F32) + fb_ref[...]
    na = _rms(ya_ref[0], ga_ref[...]).astype(BF16)
    nb = _rms(yb, gb_ref[...]).astype(BF16)
    mix = jnp.dot(na, wout_ref[:wa, :], preferred_element_type=F32)
    mix += jnp.dot(nb, wout_ref[wa:, :], preferred_element_type=F32)
    x1 = x_ref[0] + mod_ref[0, 2:3, :] * mix
    x1_ref[0] = x1
    h2 = _rms(x1, g2_ref[...]) * (1.0 + mod_ref[0, 4:5, :]) + mod_ref[0, 3:4, :]
    h2_ref[0] = h2.astype(BF16)
    h2t_ref[...] = h2.T.astype(BF16)


def _mixer_out_call(x, ya, fr, mod, fw_bd, fb, ga, gb, wout, g2, ts):
    B, S, D = x.shape
    wa = ya.shape[2]
    wf = fw_bd.shape[0]
    const = lambda shape: pl.BlockSpec(shape, lambda b, s: (0,) * len(shape))
    return pl.pallas_call(
        _mixer_out_kernel,
        grid=(B, S // ts),
        in_specs=[
            pl.BlockSpec((1, ts, D), lambda b, s: (b, s, 0)),
            pl.BlockSpec((1, ts, wa), lambda b, s: (b, s, 0)),
            pl.BlockSpec((None, ts, wf), lambda b, s: (b, s, 0)),
            pl.BlockSpec((1, N_MOD, D), lambda b, s: (b, 0, 0)),
            const(fw_bd.shape), const(fb.shape), const(ga.shape), const(gb.shape),
            const(wout.shape), const(g2.shape),
        ],
        out_specs=[
            pl.BlockSpec((1, ts, D), lambda b, s: (b, s, 0)),
            pl.BlockSpec((1, ts, D), lambda b, s: (b, s, 0)),
            pl.BlockSpec((None, D, ts), lambda b, s: (b * (S // ts) + s, 0, 0)),
        ],
        out_shape=[
            jax.ShapeDtypeStruct((B, S, D), F32),
            jax.ShapeDtypeStruct((B, S, D), BF16),
            jax.ShapeDtypeStruct((B * S // ts, D, ts), BF16),
        ],
        compiler_params=_params("parallel", "parallel"),
        name="mixer_out",
    )(x, ya, fr, mod, fw_bd, fb, ga, gb, wout, g2)


def _sort16_pairs():
    pairs = []

    def merge(lo, n, r):
        m = 2 * r
        if m < n:
            merge(lo, n, m)
            merge(lo + r, n, m)
            pairs.extend((i, i + r) for i in range(lo + r, lo + n - r, m))
        else:
            pairs.append((lo, lo + r))

    def sort(lo, n):
        if n > 1:
            sort(lo, n // 2)
            sort(lo + n // 2, n // 2)
            merge(lo, n, 1)

    sort(0, 16)
    return pairs


_SORT16 = _sort16_pairs()
_BITONIC16 = [(i, i + s) for s in (8, 4, 2, 1) for i in range(16) if (i // s) % 2 == 0]


def _exchange(v, pairs):
    v = list(v)
    for i, j in pairs:
        v[i], v[j] = jnp.maximum(v[i], v[j]), jnp.minimum(v[i], v[j])
    return v


def _top16(vals):
    groups = [(_exchange(vals[g:g + 16], _SORT16), None) for g in range(0, len(vals), 16)]
    while len(groups) > 1:
        merged = []
        for (a, da), (b, db) in zip(groups[0::2], groups[1::2]):
            keep = [jnp.maximum(a[i], b[15 - i]) for i in range(16)]
            drop = [jnp.minimum(a[i], b[15 - i]) for i in range(16)] + [d for d in (da, db) if d is not None]
            while len(drop) > 1:
                drop = [jnp.maximum(x, y) for x, y in zip(drop[0::2], drop[1::2])] + drop[len(drop) & ~1:]
            merged.append((_exchange(keep, _BITONIC16), drop[0]))
        groups = merged
    return groups[0]


def _peer_scores_kernel(h2_ref, wq_ref, a1_ref, a2_ref, k2_ref, t1_ref, e1_ref, s2_ref, w2_ref,
                        s1_scr, s2_scr, s2h_scr):
    tt = h2_ref.shape[0]
    H, K, half = k2_ref.shape
    LG = LANES
    nl = tt // LG
    nt = (((1,), (1,)), ((), ()))
    q = jnp.dot(h2_ref[...], wq_ref[...], preferred_element_type=F32).astype(BF16)
    q1 = jnp.concatenate([q[:, (2 * h) * half:(2 * h + 1) * half] for h in range(H)], axis=1)
    q2 = jnp.concatenate([q[:, (2 * h + 1) * half:(2 * h + 2) * half] for h in range(H)], axis=1)
    s1 = lax.dot_general(a1_ref[...], q1, nt, preferred_element_type=F32)
    s2 = lax.dot_general(a2_ref[...], q2, nt, preferred_element_type=F32)
    for li in range(nl):
        s1_scr[li] = s1[:, li * LG:(li + 1) * LG]
        s2_scr[li] = s2[:, li * LG:(li + 1) * LG]
    for h in range(H):
        sh = lax.dot_general(k2_ref[h], q[:, (2 * h + 1) * half:(2 * h + 2) * half], nt,
                             preferred_element_type=F32)
        for li in range(nl):
            s2h_scr[h, li] = sh[:, li * LG:(li + 1) * LG]

    def lane_group(li, carry):
        v1 = [s1_scr[li, k * H:(k + 1) * H, :] for k in range(K)]
        v2 = [s2_scr[li, k * H:(k + 1) * H, :] for k in range(K)]
        r1, d1 = _top16(v1)
        r2, d2 = _top16(v2)
        r1, r2 = r1 + [d1], r2 + [d2]
        cand = [r1[j] + r2[l] for j in range(17) for l in range(17) if (j + 1) * (l + 1) <= 17]
        pad = [jnp.full((H, LG), NEG_INF, F32)] * (-len(cand) % 16)
        top, c17 = _top16(cand + pad)
        mid = 0.5 * (top[PEER_TOPK - 1] + c17)
        m = r1[0] + r2[0]
        z = None
        for c in cand:
            term = jnp.where(c > mid, jnp.exp(c - m), 0.0)
            z = term if z is None else z + term
        scale = math.sqrt(0.5) / z
        for k in range(K):
            t1_ref[li, k] = mid - v1[k]
            e1_ref[li, k] = jnp.exp(v1[k] - r1[0]) * scale
        for h in range(H):
            sh = s2h_scr[h, li]
            s2_ref[h, li] = sh
            w2_ref[h, li] = jnp.exp(sh - r2[0][h:h + 1, :])
        return carry

    lax.fori_loop(0, nl, lane_group, 0)


def _peer_scores_call(h2, wq, a1, a2, k2, tt):
    N, D = h2.shape
    H, K, half = k2.shape
    LG = LANES
    nl = tt // LG
    const = lambda shape: pl.BlockSpec(shape, lambda i: (0,) * len(shape))
    row_spec = pl.BlockSpec((nl, K, H, LG), lambda i: (i, 0, 0, 0))
    row_shape = jax.ShapeDtypeStruct((N // LG, K, H, LG), F32)
    key_spec = pl.BlockSpec((H, nl, K, LG), lambda i: (0, i, 0, 0))
    key_shape = jax.ShapeDtypeStruct((H, N // LG, K, LG), F32)
    return pl.pallas_call(
        _peer_scores_kernel,
        grid=(N // tt,),
        in_specs=[pl.BlockSpec((tt, D), lambda i: (i, 0)), const(wq.shape), const(a1.shape), const(a2.shape),
                  const(k2.shape)],
        out_specs=[row_spec, row_spec, key_spec, key_spec],
        out_shape=[row_shape, row_shape, key_shape, key_shape],
        scratch_shapes=[pltpu.VMEM((nl, K * H, LG), F32), pltpu.VMEM((nl, K * H, LG), F32),
                        pltpu.VMEM((H, nl, K, LG), F32)],
        compiler_params=_params("parallel"),
        name="peer_scores",
    )(h2, wq, a1, a2, k2)


def _peer_dense_kernel(h2t_ref, t1_ref, e1_ref, s2_ref, w2_ref, u_ref, vt_ref, x1_ref, mod_ref, fg_ref,
                       o_ref, acc_ref, p0_ref, p1_ref, c0_ref, c1_ref,
                       t1s_ref, e1s_ref, s2s_ref, w2s_ref, h2ts_ref, *, final, nb):
    g = pl.program_id(0)
    tt = h2t_ref.shape[1]
    eb = u_ref.shape[0]
    K = PEER_KEYS
    LG = LANES

    @pl.when(g == 0)
    def _():
        acc_ref[...] = jnp.zeros_like(acc_ref)
        p1_ref[...] = jnp.zeros_like(p1_ref)
        c0_ref[...] = jnp.zeros_like(c0_ref)

    t1s_ref[...] = t1_ref[...]
    e1s_ref[...] = e1_ref[...]

    @pl.when(g % nb == 0)
    def _():
        h2ts_ref[...] = h2t_ref[...]

    @pl.when(jnp.logical_or(g == 0, (g - 1) % nb == 0))
    def _():
        s2s_ref[...] = s2_ref[...]
        w2s_ref[...] = w2_ref[...]

    def stages(p_new, p_old, c_new, c_old):
        nrow = eb // K
        dc = acc_ref.shape[0] // nrow
        QR = 32
        nq = K // QR
        units = [(li, q) for li in range(tt // LG) for q in range(nq)]
        assert len(units) == 2 * nrow
        for ui, (li, q) in enumerate(units):
            piece = ui // 2
            if ui % 2 == 0:
                rs = slice(piece * K, (piece + 1) * K)
                pre = jnp.dot(u_ref[rs, :], h2ts_ref[...], preferred_element_type=F32)
                for lj in range(tt // LG):
                    p_new[lj, rs, :] = pre[:, lj * LG:(lj + 1) * LG]
            else:
                ds = slice(piece * dc, (piece + 1) * dc)
                acc_ref[ds, :] += lax.dot_general(vt_ref[:, ds], c_old[...], (((0,), (0,)), ((), ())),
                                                  preferred_element_type=F32)
            ls = slice(li * LG, (li + 1) * LG)
            qs = slice(q * QR, (q + 1) * QR)
            gate = [None] * nrow
            for h in range(PEER_HEADS):
                s2 = s2s_ref[h, li, qs, :]
                w2 = w2s_ref[h, li, qs, :]
                for ri in range(nrow):
                    term = jnp.where(s2 > t1s_ref[li, ri, h:h + 1, :], w2, 0.0) * e1s_ref[li, ri, h:h + 1, :]
                    gate[ri] = term if gate[ri] is None else gate[ri] + term
            for ri in range(nrow):
                p = p_old[li, ri * K + q * QR:ri * K + (q + 1) * QR, :]
                coef = gate[ri] * (p * (1.0 + lax.erf(p)))
                c_new[ri * K + q * QR:ri * K + (q + 1) * QR, ls] = coef.astype(BF16)

    @pl.when(g % 2 == 0)
    def _():
        stages(p0_ref, p1_ref, c1_ref, c0_ref)

    @pl.when(g % 2 == 1)
    def _():
        stages(p1_ref, p0_ref, c0_ref, c1_ref)

    @pl.when(jnp.logical_and(g >= 2, (g - 2) % nb == nb - 1))
    def _():
        x2 = x1_ref[...] + mod_ref[0, 5:6, :] * acc_ref[...].T
        if final:
            x2 = _rms(x2, fg_ref[...])
        o_ref[...] = x2
        acc_ref[...] = jnp.zeros_like(acc_ref)


def _peer_dense_call(h2t, t1, e1, s2, w2, u, vt, x1, mod, fg, tt, eb, tokens_per_batch, layer, final):
    ntile, D, tile = h2t.shape
    assert tile == tt
    N = ntile * tt
    H, _, K, LG = s2.shape
    E = u.shape[1]
    nb = E // eb
    total = (N // tt) * nb
    tpb = tokens_per_batch // tt
    rows = eb // K
    blk = lambda g, lag: jnp.clip(g - lag, 0, total - 1)
    row_spec = pl.BlockSpec((tt // LG, rows, H, LG), lambda g: (blk(g, 1) // nb, blk(g, 1) % nb, 0, 0))
    key_spec = pl.BlockSpec((H, tt // LG, K, LG), lambda g: (0, blk(g, 1) // nb, 0, 0))
    return pl.pallas_call(
        functools.partial(_peer_dense_kernel, final=final, nb=nb),
        grid=(total + 2,),
        in_specs=[
            pl.BlockSpec((None, D, tt), lambda g: (blk(g, 0) // nb, 0, 0)),
            row_spec, row_spec, key_spec, key_spec,
            pl.BlockSpec((None, eb, D), lambda g: (layer, blk(g, 0) % nb, 0)),
            pl.BlockSpec((None, eb, D), lambda g: (layer, blk(g, 2) % nb, 0)),
            pl.BlockSpec((tt, D), lambda g: (blk(g, 2) // nb, 0)),
            pl.BlockSpec((1, N_MOD, D), lambda g: (blk(g, 2) // nb // tpb, 0, 0)),
            pl.BlockSpec((1, D), lambda g: (0, 0)),
        ],
        out_specs=pl.BlockSpec((tt, D), lambda g: (blk(g, 2) // nb, 0)),
        out_shape=jax.ShapeDtypeStruct((N, D), F32),
        scratch_shapes=[pltpu.VMEM((D, tt), F32),
                        pltpu.VMEM((tt // LG, eb, LG), F32), pltpu.VMEM((tt // LG, eb, LG), F32),
                        pltpu.VMEM((eb, tt), BF16), pltpu.VMEM((eb, tt), BF16),
                        pltpu.VMEM((tt // LG, rows, H, LG), F32), pltpu.VMEM((tt // LG, rows, H, LG), F32),
                        pltpu.VMEM((H, tt // LG, K, LG), F32), pltpu.VMEM((H, tt // LG, K, LG), F32),
                        pltpu.VMEM((D, tt), BF16)],
        compiler_params=_params("arbitrary"),
        name="peer_dense",
    )(h2t, t1, e1, s2, w2, u, vt, x1, mod, fg)


def _dft_tables(n):
    k = np.arange(n, dtype=np.int64)
    ang = 2.0 * np.pi * ((k[:, None] * k[None, :]) % n) / n
    return np.cos(ang), np.sin(ang)


def _seq_dft_mats(S, scale):
    R = 1 << (int(math.log2(S)) // 2)
    k = jnp.arange(S, dtype=jnp.int32)[:, None]
    w = 2.0 * math.pi / S
    ang_a = ((k * (jnp.arange(S // R, dtype=jnp.int32)[None, :] * R)) % S).astype(F32) * w
    ang_b = ((k * jnp.arange(R, dtype=jnp.int32)[None, :]) % S).astype(F32) * w
    ca, sa = jnp.cos(ang_a)[:, :, None], jnp.sin(ang_a)[:, :, None]
    cb, sb = jnp.cos(ang_b)[:, None, :], jnp.sin(ang_b)[:, None, :]
    cos_w = (ca * cb - sa * sb).reshape(S, S)
    sin_w = (sa * cb + ca * sb).reshape(S, S)
    return (cos_w * scale).astype(BF16), (sin_w * (-scale)).astype(BF16)


def _block_diag(blocks):
    G, a, b = blocks.shape
    eye = jnp.eye(G, dtype=blocks.dtype)
    return (eye[:, None, :, None] * blocks[:, :, None, :]).reshape(G * a, G * b)


def _key_head_rows(keys):
    H, K, d = keys.shape
    eye = jnp.eye(H, dtype=keys.dtype)
    return (jnp.transpose(keys, (1, 0, 2))[:, :, None, :] * eye[None, :, :, None]).reshape(K * H, H * d).astype(BF16)


def kernel(x, c, mod_w, mod_b, norm1_g, w_in, gmlp_ws, gmlp_bs, fnet_w, fnet_b, gain_a, gain_b, w_out,
           norm2_g, peer_wq, peer_k1, peer_k2, peer_u, peer_v, final_g):
    B, S, D = x.shape
    L = mod_w.shape[0]
    wa = gain_a.shape[1]
    wf = gain_b.shape[1]
    hd = wa // GMLP_HEADS
    gd = wf // FNET_GROUPS
    N = B * S
    ts = min(TOKEN_TILE, S)
    tt = min(TOKEN_TILE, S)
    eb = 8 * PEER_KEYS

    mod = _mod_call(c, mod_w, mod_b).reshape(L, B, N_MOD, D)

    avg = _block_diag(jnp.full((GMLP_HEADS, hd, hd), 1.0 / hd, F32)).astype(BF16)
    hmask = _block_diag(jnp.ones((GMLP_HEADS, CHUNK, hd), F32)).astype(BF16)
    cos_d, sin_d = _dft_tables(gd)
    cd = _block_diag(jnp.asarray(np.broadcast_to(cos_d, (FNET_GROUPS, gd, gd)), F32)).astype(BF16)
    sd = _block_diag(jnp.asarray(np.broadcast_to(sin_d, (FNET_GROUPS, gd, gd)), F32)).astype(BF16)
    wc, wsn = _seq_dft_mats(S, 1.0 / math.sqrt(S * gd))
    u_all = (peer_u * math.sqrt(0.5)).astype(BF16)
    vt_all = peer_v.astype(BF16)

    for l in range(L):
        ws_cat = jnp.transpose(gmlp_ws[l], (1, 0, 2)).reshape(CHUNK, GMLP_HEADS * CHUNK).astype(BF16)
        bs_full = jnp.repeat(gmlp_bs[l].T, hd, axis=1)
        ya, fc, fs = _mixer_in_call(x, mod[l], norm1_g[l][None], w_in[l].astype(BF16), avg, ws_cat, bs_full,
                                    hmask, cd, sd, ts)
        fr = _seq_dft_call(wc, wsn, fc, fs, ts)
        x1, h2, h2t = _mixer_out_call(x, ya, fr, mod[l], _block_diag(fnet_w[l]).astype(BF16),
                                 fnet_b[l].reshape(1, wf), gain_a[l][None], gain_b[l][None],
                                 w_out[l].astype(BF16), norm2_g[l][None], ts)
        h2 = h2.reshape(N, D)
        t1, e1, s2, w2 = _peer_scores_call(h2, peer_wq[l].astype(BF16), _key_head_rows(peer_k1[l]),
                                           _key_head_rows(peer_k2[l]), peer_k2[l].astype(BF16), tt)
        x = _peer_dense_call(h2t, t1, e1, s2, w2, u_all, vt_all, x1.reshape(N, D), mod[l], final_g[None],
                             tt, eb, S, l, l == L - 1).reshape(B, S, D)
    return x
```

```python
import functools
import math

import numpy as np
import jax
import jax.numpy as jnp
from jax import lax
from jax.experimental import pallas as pl
from jax.experimental.pallas import tpu as pltpu

GMLP_HEADS = 8
FNET_GROUPS = 8
CHUNK = 128
PEER_HEADS = 8
PEER_KEYS = 128
PEER_TOPK = 16
N_MOD = 6
EPS = 1e-6

LANES = 128
VMEM_LIMIT_BYTES = 56 * 1024 * 1024
TOKEN_TILE = 512
MOD_COLS = 1536

F32 = jnp.float32
BF16 = jnp.bfloat16
NEG_INF = float("-inf")


def _params(*sem):
    return pltpu.CompilerParams(dimension_semantics=sem, vmem_limit_bytes=VMEM_LIMIT_BYTES)


def _gelu(x):
    return 0.5 * x * (1.0 + lax.erf(x * (1.0 / math.sqrt(2.0))))


def _rms(x, g):
    return x * lax.rsqrt(jnp.mean(x * x, axis=-1, keepdims=True) + EPS) * g


def _mod_kernel(c_ref, w_ref, b_ref, o_ref):
    c = c_ref[...]
    sc = c * jax.nn.sigmoid(c)
    o_ref[0] = jnp.dot(sc, w_ref[0], preferred_element_type=F32) + b_ref[0]


def _mod_call(c, mod_w, mod_b):
    L, D, M = mod_w.shape
    B = c.shape[0]
    tn = MOD_COLS
    return pl.pallas_call(
        _mod_kernel,
        grid=(L, M // tn),
        in_specs=[
            pl.BlockSpec((B, D), lambda l, j: (0, 0)),
            pl.BlockSpec((1, D, tn), lambda l, j: (l, 0, j)),
            pl.BlockSpec((1, 1, tn), lambda l, j: (l, 0, j)),
        ],
        out_specs=pl.BlockSpec((1, B, tn), lambda l, j: (l, 0, j)),
        out_shape=jax.ShapeDtypeStruct((L, B, M), F32),
        compiler_params=_params("parallel", "parallel"),
        name="adaln_mod",
    )(c, mod_w, mod_b.reshape(L, 1, M))


def _mixer_in_kernel(x_ref, mod_ref, g_ref, win_ref, avg_ref, ws_ref, bs_ref, hmask_ref, cd_ref, sd_ref,
                     ya_ref, fc_ref, fs_ref):
    ts = x_ref.shape[1]
    wa = ya_ref.shape[2]
    x = x_ref[0]
    h = _rms(x, g_ref[...]) * (1.0 + mod_ref[0, 1:2, :]) + mod_ref[0, 0:1, :]
    z = jnp.dot(h.astype(BF16), win_ref[...], preferred_element_type=F32)
    u = _gelu(z[:, :wa])
    gv = _gelu(z[:, wa:2 * wa])
    zf = z[:, 2 * wa:].astype(BF16)
    mu = jnp.dot(gv.astype(BF16), avg_ref[...], preferred_element_type=F32)
    dv = gv - mu
    var = jnp.dot((dv * dv).astype(BF16), avg_ref[...], preferred_element_type=F32)
    v = (dv * lax.rsqrt(var + EPS)).astype(BF16)
    hmask = hmask_ref[...]
    for ci in range(ts // CHUNK):
        vc = v[ci * CHUNK:(ci + 1) * CHUNK, :]
        vstack = jnp.concatenate([vc] * GMLP_HEADS, axis=0) * hmask
        sv = jnp.dot(ws_ref[...], vstack, preferred_element_type=F32) + bs_ref[...]
        ya_ref[0, ci * CHUNK:(ci + 1) * CHUNK, :] = u[ci * CHUNK:(ci + 1) * CHUNK, :] * sv
    fc_ref[...] = jnp.dot(zf, cd_ref[...], preferred_element_type=F32).astype(BF16)
    fs_ref[...] = jnp.dot(zf, sd_ref[...], preferred_element_type=F32).astype(BF16)


def _mixer_in_call(x, mod, g, win, avg, ws_cat, bs_full, hmask, cd, sd, ts):
    B, S, D = x.shape
    wa = avg.shape[0]
    wf = cd.shape[0]
    const = lambda shape: pl.BlockSpec(shape, lambda b, s: (0,) * len(shape))
    return pl.pallas_call(
        _mixer_in_kernel,
        grid=(B, S // ts),
        in_specs=[
            pl.BlockSpec((1, ts, D), lambda b, s: (b, s, 0)),
            pl.BlockSpec((1, N_MOD, D), lambda b, s: (b, 0, 0)),
            const(g.shape), const(win.shape), const(avg.shape), const(ws_cat.shape),
            const(bs_full.shape), const(hmask.shape), const(cd.shape), const(sd.shape),
        ],
        out_specs=[
            pl.BlockSpec((1, ts, wa), lambda b, s: (b, s, 0)),
            pl.BlockSpec((None, ts, wf), lambda b, s: (b, s, 0)),
            pl.BlockSpec((None, ts, wf), lambda b, s: (b, s, 0)),
        ],
        out_shape=[
            jax.ShapeDtypeStruct((B, S, wa), F32),
            jax.ShapeDtypeStruct((B, S, wf), BF16),
            jax.ShapeDtypeStruct((B, S, wf), BF16),
        ],
        compiler_params=_params("parallel", "parallel"),
        name="mixer_in",
    )(x, mod, g, win, avg, ws_cat, bs_full, hmask, cd, sd)


def _seq_dft_kernel(wc_ref, wsn_ref, fc_ref, fs_ref, o_ref):
    acc = jnp.dot(wc_ref[...], fc_ref[...], preferred_element_type=F32)
    acc += jnp.dot(wsn_ref[...], fs_ref[...], preferred_element_type=F32)
    o_ref[...] = acc.astype(o_ref.dtype)


def _seq_dft_call(wc, wsn, fc, fs, tm):
    B, S, wf = fc.shape
    return pl.pallas_call(
        _seq_dft_kernel,
        grid=(S // tm, B),
        in_specs=[
            pl.BlockSpec((tm, S), lambda i, j: (i, 0)),
            pl.BlockSpec((tm, S), lambda i, j: (i, 0)),
            pl.BlockSpec((None, S, wf), lambda i, j: (j, 0, 0)),
            pl.BlockSpec((None, S, wf), lambda i, j: (j, 0, 0)),
        ],
        out_specs=pl.BlockSpec((None, tm, wf), lambda i, j: (j, i, 0)),
        out_shape=jax.ShapeDtypeStruct((B, S, wf), BF16),
        compiler_params=_params("parallel", "parallel"),
        name="seq_dft",
    )(wc, wsn, fc, fs)


def _mixer_out_kernel(x_ref, ya_ref, fr_ref, mod_ref, fw_ref, fb_ref, ga_ref, gb_ref, wout_ref, g2_ref,
                      x1_ref, h2_ref, h2t_ref):
    wa = ya_ref.shape[2]
    yb = jnp.dot(fr_ref[...], fw_ref[...], preferred_element_type=F32) + fb_ref[...]
    na = _rms(ya_ref[0], ga_ref[...]).astype(BF16)
    nb = _rms(yb, gb_ref[...]).astype(BF16)
    mix = jnp.dot(na, wout_ref[:wa, :], preferred_element_type=F32)
    mix += jnp.dot(nb, wout_ref[wa:, :], preferred_element_type=F32)
    x1 = x_ref[0] + mod_ref[0, 2:3, :] * mix
    x1_ref[0] = x1
    h2 = _rms(x1, g2_ref[...]) * (1.0 + mod_ref[0, 4:5, :]) + mod_ref[0, 3:4, :]
    h2_ref[0] = h2.astype(BF16)
    h2t_ref[...] = h2.T.astype(BF16)


def _mixer_out_call(x, ya, fr, mod, fw_bd, fb, ga, gb, wout, g2, ts):
    B, S, D = x.shape
    wa = ya.shape[2]
    wf = fw_bd.shape[0]
    const = lambda shape: pl.BlockSpec(shape, lambda b, s: (0,) * len(shape))
    return pl.pallas_call(
        _mixer_out_kernel,
        grid=(B, S // ts),
        in_specs=[
            pl.BlockSpec((1, ts, D), lambda b, s: (b, s, 0)),
            pl.BlockSpec((1, ts, wa), lambda b, s: (b, s, 0)),
            pl.BlockSpec((None, ts, wf), lambda b, s: (b, s, 0)),
            pl.BlockSpec((1, N_MOD, D), lambda b, s: (b, 0, 0)),
            const(fw_bd.shape), const(fb.shape), const(ga.shape), const(gb.shape),
            const(wout.shape), const(g2.shape),
        ],
        out_specs=[
            pl.BlockSpec((1, ts, D), lambda b, s: (b, s, 0)),
            pl.BlockSpec((1, ts, D), lambda b, s: (b, s, 0)),
            pl.BlockSpec((None, D, ts), lambda b, s: (b * (S // ts) + s, 0, 0)),
        ],
        out_shape=[
            jax.ShapeDtypeStruct((B, S, D), F32),
            jax.ShapeDtypeStruct((B, S, D), BF16),
            jax.ShapeDtypeStruct((B * S // ts, D, ts), BF16),
        ],
        compiler_params=_params("parallel", "parallel"),
        name="mixer_out",
    )(x, ya, fr, mod, fw_bd, fb, ga, gb, wout, g2)


def _sort16_pairs():
    pairs = []

    def merge(lo, n, r):
        m = 2 * r
        if m < n:
            merge(lo, n, m)
            merge(lo + r, n, m)
            pairs.extend((i, i + r) for i in range(lo + r, lo + n - r, m))
        else:
            pairs.append((lo, lo + r))

    def sort(lo, n):
        if n > 1:
            sort(lo, n // 2)
            sort(lo + n // 2, n // 2)
            merge(lo, n, 1)

    sort(0, 16)
    return pairs


_SORT16 = _sort16_pairs()
_BITONIC16 = [(i, i + s) for s in (8, 4, 2, 1) for i in range(16) if (i // s) % 2 == 0]


def _exchange(v, pairs):
    v = list(v)
    for i, j in pairs:
        v[i], v[j] = jnp.maximum(v[i], v[j]), jnp.minimum(v[i], v[j])
    return v


def _top16(vals):
    groups = [(_exchange(vals[g:g + 16], _SORT16), None) for g in range(0, len(vals), 16)]
    while len(groups) > 1:
        merged = []
        for (a, da), (b, db) in zip(groups[0::2], groups[1::2]):
            keep = [jnp.maximum(a[i], b[15 - i]) for i in range(16)]
            drop = [jnp.minimum(a[i], b[15 - i]) for i in range(16)] + [d for d in (da, db) if d is not None]
            while len(drop) > 1:
                drop = [jnp.maximum(x, y) for x, y in zip(drop[0::2], drop[1::2])] + drop[len(drop) & ~1:]
            merged.append((_exchange(keep, _BITONIC16), drop[0]))
        groups = merged
    return groups[0]


def _peer_scores_kernel(h2_ref, wq_ref, a1_ref, a2_ref, k2_ref, te_ref, s2_ref, w2_ref,
                        s1_scr, s2_scr, s2h_scr):
    tt = h2_ref.shape[0]
    H, K, half = k2_ref.shape
    LG = LANES
    nl = tt // LG
    nt = (((1,), (1,)), ((), ()))
    q = jnp.dot(h2_ref[...], wq_ref[...], preferred_element_type=F32).astype(BF16)
    q1 = jnp.concatenate([q[:, (2 * h) * half:(2 * h + 1) * half] for h in range(H)], axis=1)
    q2 = jnp.concatenate([q[:, (2 * h + 1) * half:(2 * h + 2) * half] for h in range(H)], axis=1)
    s1 = lax.dot_general(a1_ref[...], q1, nt, preferred_element_type=F32)
    s2 = lax.dot_general(a2_ref[...], q2, nt, preferred_element_type=F32)
    for li in range(nl):
        s1_scr[li] = s1[:, li * LG:(li + 1) * LG]
        s2_scr[li] = s2[:, li * LG:(li + 1) * LG]
    for h in range(H):
        sh = lax.dot_general(k2_ref[h], q[:, (2 * h + 1) * half:(2 * h + 2) * half], nt,
                             preferred_element_type=F32)
        for li in range(nl):
            s2h_scr[h, li] = sh[:, li * LG:(li + 1) * LG]

    def lane_group(li, carry):
        v1 = [s1_scr[li, k * H:(k + 1) * H, :] for k in range(K)]
        v2 = [s2_scr[li, k * H:(k + 1) * H, :] for k in range(K)]
        r1, d1 = _top16(v1)
        r2, d2 = _top16(v2)
        r1, r2 = r1 + [d1], r2 + [d2]
        cand = [r1[j] + r2[l] for j in range(17) for l in range(17) if (j + 1) * (l + 1) <= 17]
        pad = [jnp.full((H, LG), NEG_INF, F32)] * (-len(cand) % 16)
        top, c17 = _top16(cand + pad)
        mid = 0.5 * (top[PEER_TOPK - 1] + c17)
        m = r1[0] + r2[0]
        z = None
        for c in cand:
            term = jnp.where(c > mid, jnp.exp(c - m), 0.0)
            z = term if z is None else z + term
        scale = math.sqrt(0.5) / z
        for k in range(K):
            te_ref[li, k, 0] = mid - v1[k]
            te_ref[li, k, 1] = jnp.exp(v1[k] - r1[0]) * scale
        for h in range(H):
            sh = s2h_scr[h, li]
            s2_ref[h, li] = sh
            w2_ref[h, li] = jnp.exp(sh - r2[0][h:h + 1, :])
        return carry

    lax.fori_loop(0, nl, lane_group, 0)


def _peer_scores_call(h2, wq, a1, a2, k2, tt):
    N, D = h2.shape
    H, K, half = k2.shape
    LG = LANES
    nl = tt // LG
    const = lambda shape: pl.BlockSpec(shape, lambda i: (0,) * len(shape))
    row_spec = pl.BlockSpec((nl, K, 2, H, LG), lambda i: (i, 0, 0, 0, 0))
    row_shape = jax.ShapeDtypeStruct((N // LG, K, 2, H, LG), F32)
    key_spec = pl.BlockSpec((H, nl, K, LG), lambda i: (0, i, 0, 0))
    key_shape = jax.ShapeDtypeStruct((H, N // LG, K, LG), F32)
    return pl.pallas_call(
        _peer_scores_kernel,
        grid=(N // tt,),
        in_specs=[pl.BlockSpec((tt, D), lambda i: (i, 0)), const(wq.shape), const(a1.shape), const(a2.shape),
                  const(k2.shape)],
        out_specs=[row_spec, key_spec, key_spec],
        out_shape=[row_shape, key_shape, key_shape],
        scratch_shapes=[pltpu.VMEM((nl, K * H, LG), F32), pltpu.VMEM((nl, K * H, LG), F32),
                        pltpu.VMEM((H, nl, K, LG), F32)],
        compiler_params=_params("parallel"),
        name="peer_scores",
    )(h2, wq, a1, a2, k2)


def _peer_dense_kernel(h2t_ref, te_ref, s2_ref, w2_ref, u_ref, vt_ref, x1_ref, mod_ref, fg_ref,
                       o_ref, acc_ref, p0_ref, p1_ref, c0_ref, c1_ref,
                       tes_ref, s2s_ref, w2s_ref, h2ts_ref, *, final, nb):
    g = pl.program_id(0)
    tt = h2t_ref.shape[1]
    eb = u_ref.shape[0]
    K = PEER_KEYS
    LG = LANES

    @pl.when(g == 0)
    def _():
        acc_ref[...] = jnp.zeros_like(acc_ref)
        p1_ref[...] = jnp.zeros_like(p1_ref)
        c0_ref[...] = jnp.zeros_like(c0_ref)

    tes_ref[...] = te_ref[...]

    @pl.when(g % nb == 0)
    def _():
        h2ts_ref[...] = h2t_ref[...]

    @pl.when(jnp.logical_or(g == 0, (g - 1) % nb == 0))
    def _():
        s2s_ref[...] = s2_ref[...]
        w2s_ref[...] = w2_ref[...]

    def stages(p_new, p_old, c_new, c_old):
        nrow = eb // K
        dc = acc_ref.shape[0] // nrow
        QR = 32
        nq = K // QR
        units = [(li, q) for li in range(tt // LG) for q in range(nq)]
        assert len(units) == 2 * nrow
        for ui, (li, q) in enumerate(units):
            piece = ui // 2
            if ui % 2 == 0:
                rs = slice(piece * K, (piece + 1) * K)
                pre = jnp.dot(u_ref[rs, :], h2ts_ref[...], preferred_element_type=F32)
                for lj in range(tt // LG):
                    p_new[lj, rs, :] = pre[:, lj * LG:(lj + 1) * LG]
            else:
                ds = slice(piece * dc, (piece + 1) * dc)
                acc_ref[ds, :] += lax.dot_general(vt_ref[:, ds], c_old[...], (((0,), (0,)), ((), ())),
                                                  preferred_element_type=F32)
            ls = slice(li * LG, (li + 1) * LG)
            qs = slice(q * QR, (q + 1) * QR)
            gate = [None] * nrow
            for h in range(PEER_HEADS):
                s2 = s2s_ref[h, li, qs, :]
                w2 = w2s_ref[h, li, qs, :]
                for ri in range(nrow):
                    term = jnp.where(s2 > tes_ref[li, ri, 0, h:h + 1, :], w2, 0.0) * tes_ref[li, ri, 1, h:h + 1, :]
                    gate[ri] = term if gate[ri] is None else gate[ri] + term
            for ri in range(nrow):
                p = p_old[li, ri * K + q * QR:ri * K + (q + 1) * QR, :]
                coef = gate[ri] * (p * (1.0 + lax.erf(p)))
                c_new[ri * K + q * QR:ri * K + (q + 1) * QR, ls] = coef.astype(BF16)

    @pl.when(g % 2 == 0)
    def _():
        stages(p0_ref, p1_ref, c1_ref, c0_ref)

    @pl.when(g % 2 == 1)
    def _():
        stages(p1_ref, p0_ref, c0_ref, c1_ref)

    @pl.when(jnp.logical_and(g >= 2, (g - 2) % nb == nb - 1))
    def _():
        x2 = x1_ref[...] + mod_ref[0, 5:6, :] * acc_ref[...].T
        if final:
            x2 = _rms(x2, fg_ref[...])
        o_ref[...] = x2
        acc_ref[...] = jnp.zeros_like(acc_ref)


def _peer_dense_call(h2t, te, s2, w2, u, vt, x1, mod, fg, tt, eb, tokens_per_batch, layer, final):
    ntile, D, tile = h2t.shape
    assert tile == tt
    N = ntile * tt
    H, _, K, LG = s2.shape
    E = u.shape[1]
    nb = E // eb
    total = (N // tt) * nb
    tpb = tokens_per_batch // tt
    rows = eb // K
    blk = lambda g, lag: jnp.clip(g - lag, 0, total - 1)
    row_spec = pl.BlockSpec((tt // LG, rows, 2, H, LG), lambda g: (blk(g, 1) // nb, blk(g, 1) % nb, 0, 0, 0))
    key_spec = pl.BlockSpec((H, tt // LG, K, LG), lambda g: (0, blk(g, 1) // nb, 0, 0))
    return pl.pallas_call(
        functools.partial(_peer_dense_kernel, final=final, nb=nb),
        grid=(total + 2,),
        in_specs=[
            pl.BlockSpec((None, D, tt), lambda g: (blk(g, 0) // nb, 0, 0)),
            row_spec, key_spec, key_spec,
            pl.BlockSpec((None, eb, D), lambda g: (layer, blk(g, 0) % nb, 0)),
            pl.BlockSpec((None, eb, D), lambda g: (layer, blk(g, 2) % nb, 0)),
            pl.BlockSpec((tt, D), lambda g: (blk(g, 2) // nb, 0)),
            pl.BlockSpec((1, N_MOD, D), lambda g: (blk(g, 2) // nb // tpb, 0, 0)),
            pl.BlockSpec((1, D), lambda g: (0, 0)),
        ],
        out_specs=pl.BlockSpec((tt, D), lambda g: (blk(g, 2) // nb, 0)),
        out_shape=jax.ShapeDtypeStruct((N, D), F32),
        scratch_shapes=[pltpu.VMEM((D, tt), F32),
                        pltpu.VMEM((tt // LG, eb, LG), F32), pltpu.VMEM((tt // LG, eb, LG), F32),
                        pltpu.VMEM((eb, tt), BF16), pltpu.VMEM((eb, tt), BF16),
                        pltpu.VMEM((tt // LG, rows, 2, H, LG), F32),
                        pltpu.VMEM((H, tt // LG, K, LG), F32), pltpu.VMEM((H, tt // LG, K, LG), F32),
                        pltpu.VMEM((D, tt), BF16)],
        compiler_params=_params("arbitrary"),
        name="peer_dense",
    )(h2t, te, s2, w2, u, vt, x1, mod, fg)


def _dft_tables(n):
    k = np.arange(n, dtype=np.int64)
    ang = 2.0 * np.pi * ((k[:, None] * k[None, :]) % n) / n
    return np.cos(ang), np.sin(ang)


def _seq_dft_mats(S, scale):
    R = 1 << (int(math.log2(S)) // 2)
    k = jnp.arange(S, dtype=jnp.int32)[None, :]
    w = 2.0 * math.pi / S
    ang_a = (((jnp.arange(S // R, dtype=jnp.int32)[:, None] * R) * k) % S).astype(F32) * w
    ang_b = ((jnp.arange(R, dtype=jnp.int32)[:, None] * k) % S).astype(F32) * w
    ca, sa = jnp.cos(ang_a)[:, None, :], jnp.sin(ang_a)[:, None, :]
    cb, sb = jnp.cos(ang_b)[None, :, :], jnp.sin(ang_b)[None, :, :]
    cos_w = (ca * cb - sa * sb).reshape(S, S)
    sin_w = (sa * cb + ca * sb).reshape(S, S)
    return (cos_w * scale).astype(BF16), (sin_w * (-scale)).astype(BF16)


def _block_diag(blocks):
    G, a, b = blocks.shape
    eye = jnp.eye(G, dtype=blocks.dtype)
    return (eye[:, None, :, None] * blocks[:, :, None, :]).reshape(G * a, G * b)


def _key_head_rows(keys):
    H, K, d = keys.shape
    eye = jnp.eye(H, dtype=keys.dtype)
    return (jnp.transpose(keys, (1, 0, 2))[:, :, None, :] * eye[None, :, :, None]).reshape(K * H, H * d).astype(BF16)


def kernel(x, c, mod_w, mod_b, norm1_g, w_in, gmlp_ws, gmlp_bs, fnet_w, fnet_b, gain_a, gain_b, w_out,
           norm2_g, peer_wq, peer_k1, peer_k2, peer_u, peer_v, final_g):
    B, S, D = x.shape
    L = mod_w.shape[0]
    wa = gain_a.shape[1]
    wf = gain_b.shape[1]
    hd = wa // GMLP_HEADS
    gd = wf // FNET_GROUPS
    N = B * S
    ts = min(TOKEN_TILE, S)
    tt = min(TOKEN_TILE, S)
    eb = 8 * PEER_KEYS

    mod = _mod_call(c, mod_w, mod_b).reshape(L, B, N_MOD, D)

    avg = _block_diag(jnp.full((GMLP_HEADS, hd, hd), 1.0 / hd, F32)).astype(BF16)
    hmask = _block_diag(jnp.ones((GMLP_HEADS, CHUNK, hd), F32)).astype(BF16)
    cos_d, sin_d = _dft_tables(gd)
    cd = _block_diag(jnp.asarray(np.broadcast_to(cos_d, (FNET_GROUPS, gd, gd)), F32)).astype(BF16)
    sd = _block_diag(jnp.asarray(np.broadcast_to(sin_d, (FNET_GROUPS, gd, gd)), F32)).astype(BF16)
    wc, wsn = _seq_dft_mats(S, 1.0 / math.sqrt(S * gd))
    u_all = (peer_u * math.sqrt(0.5)).astype(BF16)
    vt_all = peer_v.astype(BF16)

    for l in range(L):
        ws_cat = jnp.transpose(gmlp_ws[l], (1, 0, 2)).reshape(CHUNK, GMLP_HEADS * CHUNK).astype(BF16)
        bs_full = jnp.repeat(gmlp_bs[l].T, hd, axis=1)
        ya, fc, fs = _mixer_in_call(x, mod[l], norm1_g[l][None], w_in[l].astype(BF16), avg, ws_cat, bs_full,
                                    hmask, cd, sd, ts)
        fr = _seq_dft_call(wc, wsn, fc, fs, ts)
        x1, h2, h2t = _mixer_out_call(x, ya, fr, mod[l], _block_diag(fnet_w[l]).astype(BF16),
                                 fnet_b[l].reshape(1, wf), gain_a[l][None], gain_b[l][None],
                                 w_out[l].astype(BF16), norm2_g[l][None], ts)
        h2 = h2.reshape(N, D)
        te, s2, w2 = _peer_scores_call(h2, peer_wq[l].astype(BF16), _key_head_rows(peer_k1[l]),
                                           _key_head_rows(peer_k2[l]), peer_k2[l].astype(BF16), tt)
        x = _peer_dense_call(h2t, te, s2, w2, u_all, vt_all, x1.reshape(N, D), mod[l], final_g[None],
                             tt, eb, S, l, l == L - 1).reshape(B, S, D)
    return x
```

```python
import functools
import math

import numpy as np
import jax
import jax.numpy as jnp
from jax import lax
from jax.experimental import pallas as pl
from jax.experimental.pallas import tpu as pltpu

GMLP_HEADS = 8
FNET_GROUPS = 8
CHUNK = 128
PEER_HEADS = 8
PEER_KEYS = 128
PEER_TOPK = 16
N_MOD = 6
EPS = 1e-6

LANES = 128
VMEM_LIMIT_BYTES = 56 * 1024 * 1024
TOKEN_TILE = 512
MOD_COLS = 1536

F32 = jnp.float32
BF16 = jnp.bfloat16
NEG_INF = float("-inf")


def _params(*sem):
    return pltpu.CompilerParams(dimension_semantics=sem, vmem_limit_bytes=VMEM_LIMIT_BYTES)


def _gelu(x):
    return 0.5 * x * (1.0 + lax.erf(x * (1.0 / math.sqrt(2.0))))


def _rms(x, g):
    return x * lax.rsqrt(jnp.mean(x * x, axis=-1, keepdims=True) + EPS) * g


def _mod_kernel(c_ref, w_ref, b_ref, o_ref):
    c = c_ref[...]
    sc = c * jax.nn.sigmoid(c)
    o_ref[0] = jnp.dot(sc, w_ref[0], preferred_element_type=F32) + b_ref[0]


def _mod_call(c, mod_w, mod_b):
    L, D, M = mod_w.shape
    B = c.shape[0]
    tn = MOD_COLS
    return pl.pallas_call(
        _mod_kernel,
        grid=(L, M // tn),
        in_specs=[
            pl.BlockSpec((B, D), lambda l, j: (0, 0)),
            pl.BlockSpec((1, D, tn), lambda l, j: (l, 0, j)),
            pl.BlockSpec((1, 1, tn), lambda l, j: (l, 0, j)),
        ],
        out_specs=pl.BlockSpec((1, B, tn), lambda l, j: (l, 0, j)),
        out_shape=jax.ShapeDtypeStruct((L, B, M), F32),
        compiler_params=_params("parallel", "parallel"),
        name="adaln_mod",
    )(c, mod_w, mod_b.reshape(L, 1, M))


def _mixer_in_kernel(x_ref, mod_ref, g_ref, win_ref, avg_ref, ws_ref, bs_ref, hmask_ref, cd_ref, sd_ref,
                     ya_ref, fc_ref, fs_ref):
    ts = x_ref.shape[1]
    wa = ya_ref.shape[2]
    x = x_ref[0]
    h = _rms(x, g_ref[...]) * (1.0 + mod_ref[0, 1:2, :]) + mod_ref[0, 0:1, :]
    z = jnp.dot(h.astype(BF16), win_ref[...], preferred_element_type=F32)
    u = _gelu(z[:, :wa])
    gv = _gelu(z[:, wa:2 * wa])
    zf = z[:, 2 * wa:].astype(BF16)
    mu = jnp.dot(gv.astype(BF16), avg_ref[...], preferred_element_type=F32)
    dv = gv - mu
    var = jnp.dot((dv * dv).astype(BF16), avg_ref[...], preferred_element_type=F32)
    v = (dv * lax.rsqrt(var + EPS)).astype(BF16)
    hmask = hmask_ref[...]
    for ci in range(ts // CHUNK):
        vc = v[ci * CHUNK:(ci + 1) * CHUNK, :]
        vstack = jnp.concatenate([vc] * GMLP_HEADS, axis=0) * hmask
        sv = jnp.dot(ws_ref[...], vstack, preferred_element_type=F32) + bs_ref[...]
        ya_ref[0, ci * CHUNK:(ci + 1) * CHUNK, :] = u[ci * CHUNK:(ci + 1) * CHUNK, :] * sv
    fc_ref[...] = jnp.dot(zf, cd_ref[...], preferred_element_type=F32).astype(BF16)
    fs_ref[...] = jnp.dot(zf, sd_ref[...], preferred_element_type=F32).astype(BF16)


def _mixer_in_call(x, mod, g, win, avg, ws_cat, bs_full, hmask, cd, sd, ts):
    B, S, D = x.shape
    wa = avg.shape[0]
    wf = cd.shape[0]
    const = lambda shape: pl.BlockSpec(shape, lambda b, s: (0,) * len(shape))
    return pl.pallas_call(
        _mixer_in_kernel,
        grid=(B, S // ts),
        in_specs=[
            pl.BlockSpec((1, ts, D), lambda b, s: (b, s, 0)),
            pl.BlockSpec((1, N_MOD, D), lambda b, s: (b, 0, 0)),
            const(g.shape), const(win.shape), const(avg.shape), const(ws_cat.shape),
            const(bs_full.shape), const(hmask.shape), const(cd.shape), const(sd.shape),
        ],
        out_specs=[
            pl.BlockSpec((1, ts, wa), lambda b, s: (b, s, 0)),
            pl.BlockSpec((None, ts, wf), lambda b, s: (b, s, 0)),
            pl.BlockSpec((None, ts, wf), lambda b, s: (b, s, 0)),
        ],
        out_shape=[
            jax.ShapeDtypeStruct((B, S, wa), F32),
            jax.ShapeDtypeStruct((B, S, wf), BF16),
            jax.ShapeDtypeStruct((B, S, wf), BF16),
        ],
        compiler_params=_params("parallel", "parallel"),
        name="mixer_in",
    )(x, mod, g, win, avg, ws_cat, bs_full, hmask, cd, sd)


def _seq_dft_kernel(wc_ref, wsn_ref, fc_ref, fs_ref, o_ref):
    acc = jnp.dot(wc_ref[...], fc_ref[...], preferred_element_type=F32)
    acc += jnp.dot(wsn_ref[...], fs_ref[...], preferred_element_type=F32)
    o_ref[...] = acc.astype(o_ref.dtype)


def _seq_dft_call(wc, wsn, fc, fs, tm):
    B, S, wf = fc.shape
    return pl.pallas_call(
        _seq_dft_kernel,
        grid=(S // tm, B),
        in_specs=[
            pl.BlockSpec((tm, S), lambda i, j: (i, 0)),
            pl.BlockSpec((tm, S), lambda i, j: (i, 0)),
            pl.BlockSpec((None, S, wf), lambda i, j: (j, 0, 0)),
            pl.BlockSpec((None, S, wf), lambda i, j: (j, 0, 0)),
        ],
        out_specs=pl.BlockSpec((None, tm, wf), lambda i, j: (j, i, 0)),
        out_shape=jax.ShapeDtypeStruct((B, S, wf), BF16),
        compiler_params=_params("parallel", "parallel"),
        name="seq_dft",
    )(wc, wsn, fc, fs)


def _mixer_out_kernel(x_ref, ya_ref, fr_ref, mod_ref, fw_ref, fb_ref, ga_ref, gb_ref, wout_ref, g2_ref,
                      x1_ref, h2_ref, h2t_ref):
    wa = ya_ref.shape[2]
    yb = jnp.dot(fr_ref[...], fw_ref[...], preferred_element_type=F32) + fb_ref[...]
    na = _rms(ya_ref[0], ga_ref[...]).astype(BF16)
    nb = _rms(yb, gb_ref[...]).astype(BF16)
    mix = jnp.dot(na, wout_ref[:wa, :], preferred_element_type=F32)
    mix += jnp.dot(nb, wout_ref[wa:, :], preferred_element_type=F32)
    x1 = x_ref[0] + mod_ref[0, 2:3, :] * mix
    x1_ref[0] = x1
    h2 = _rms(x1, g2_ref[...]) * (1.0 + mod_ref[0, 4:5, :]) + mod_ref[0, 3:4, :]
    h2_ref[0] = h2.astype(BF16)
    h2t_ref[...] = h2.T.astype(BF16)


def _mixer_out_call(x, ya, fr, mod, fw_bd, fb, ga, gb, wout, g2, ts):
    B, S, D = x.shape
    wa = ya.shape[2]
    wf = fw_bd.shape[0]
    const = lambda shape: pl.BlockSpec(shape, lambda b, s: (0,) * len(shape))
    return pl.pallas_call(
        _mixer_out_kernel,
        grid=(B, S // ts),
        in_specs=[
            pl.BlockSpec((1, ts, D), lambda b, s: (b, s, 0)),
            pl.BlockSpec((1, ts, wa), lambda b, s: (b, s, 0)),
            pl.BlockSpec((None, ts, wf), lambda b, s: (b, s, 0)),
            pl.BlockSpec((1, N_MOD, D), lambda b, s: (b, 0, 0)),
            const(fw_bd.shape), const(fb.shape), const(ga.shape), const(gb.shape),
            const(wout.shape), const(g2.shape),
        ],
        out_specs=[
            pl.BlockSpec((1, ts, D), lambda b, s: (b, s, 0)),
            pl.BlockSpec((1, ts, D), lambda b, s: (b, s, 0)),
            pl.BlockSpec((None, D, ts), lambda b, s: (b * (S // ts) + s, 0, 0)),
        ],
        out_shape=[
            jax.ShapeDtypeStruct((B, S, D), F32),
            jax.ShapeDtypeStruct((B, S, D), BF16),
            jax.ShapeDtypeStruct((B * S // ts, D, ts), BF16),
        ],
        compiler_params=_params("parallel", "parallel"),
        name="mixer_out",
    )(x, ya, fr, mod, fw_bd, fb, ga, gb, wout, g2)


def _sort16_pairs():
    pairs = []

    def merge(lo, n, r):
        m = 2 * r
        if m < n:
            merge(lo, n, m)
            merge(lo + r, n, m)
            pairs.extend((i, i + r) for i in range(lo + r, lo + n - r, m))
        else:
            pairs.append((lo, lo + r))

    def sort(lo, n):
        if n > 1:
            sort(lo, n // 2)
            sort(lo + n // 2, n // 2)
            merge(lo, n, 1)

    sort(0, 16)
    return pairs


_SORT16 = _sort16_pairs()
_BITONIC16 = [(i, i + s) for s in (8, 4, 2, 1) for i in range(16) if (i // s) % 2 == 0]


def _exchange(v, pairs):
    v = list(v)
    for i, j in pairs:
        v[i], v[j] = jnp.maximum(v[i], v[j]), jnp.minimum(v[i], v[j])
    return v


def _top16(vals):
    groups = [(_exchange(vals[g:g + 16], _SORT16), None) for g in range(0, len(vals), 16)]
    while len(groups) > 1:
        merged = []
        for (a, da), (b, db) in zip(groups[0::2], groups[1::2]):
            keep = [jnp.maximum(a[i], b[15 - i]) for i in range(16)]
            drop = [jnp.minimum(a[i], b[15 - i]) for i in range(16)] + [d for d in (da, db) if d is not None]
            while len(drop) > 1:
                drop = [jnp.maximum(x, y) for x, y in zip(drop[0::2], drop[1::2])] + drop[len(drop) & ~1:]
            merged.append((_exchange(keep, _BITONIC16), drop[0]))
        groups = merged
    return groups[0]


def _peer_scores_kernel(h2_ref, wq_ref, a1_ref, a2_ref, k2_ref, te_ref, s2_ref, w2_ref,
                        s1_scr, s2_scr, s2h_scr):
    tt = h2_ref.shape[0]
    H, K, half = k2_ref.shape
    LG = LANES
    nl = tt // LG
    nt = (((1,), (1,)), ((), ()))
    q = jnp.dot(h2_ref[...], wq_ref[...], preferred_element_type=F32).astype(BF16)
    q1 = jnp.concatenate([q[:, (2 * h) * half:(2 * h + 1) * half] for h in range(H)], axis=1)
    q2 = jnp.concatenate([q[:, (2 * h + 1) * half:(2 * h + 2) * half] for h in range(H)], axis=1)
    s1 = lax.dot_general(a1_ref[...], q1, nt, preferred_element_type=F32)
    s2 = lax.dot_general(a2_ref[...], q2, nt, preferred_element_type=F32)
    for li in range(nl):
        s1_scr[li] = s1[:, li * LG:(li + 1) * LG]
        s2_scr[li] = s2[:, li * LG:(li + 1) * LG]
    for h in range(H):
        sh = lax.dot_general(k2_ref[h], q[:, (2 * h + 1) * half:(2 * h + 2) * half], nt,
                             preferred_element_type=F32)
        for li in range(nl):
            s2h_scr[h, li] = sh[:, li * LG:(li + 1) * LG]

    def lane_group(li, carry):
        v1 = [s1_scr[li, k * H:(k + 1) * H, :] for k in range(K)]
        v2 = [s2_scr[li, k * H:(k + 1) * H, :] for k in range(K)]
        r1, d1 = _top16(v1)
        r2, d2 = _top16(v2)
        r1, r2 = r1 + [d1], r2 + [d2]
        cand = [r1[j] + r2[l] for j in range(17) for l in range(17) if (j + 1) * (l + 1) <= 17]
        pad = [jnp.full((H, LG), NEG_INF, F32)] * (-len(cand) % 16)
        top, c17 = _top16(cand + pad)
        mid = 0.5 * (top[PEER_TOPK - 1] + c17)
        m = r1[0] + r2[0]
        z = None
        for c in cand:
            term = jnp.where(c > mid, jnp.exp(c - m), 0.0)
            z = term if z is None else z + term
        scale = math.sqrt(0.5) / z
        for k in range(K):
            te_ref[li, k, 0] = mid - v1[k]
            te_ref[li, k, 1] = jnp.exp(v1[k] - r1[0]) * scale
        for h in range(H):
            sh = s2h_scr[h, li]
            s2_ref[h, li] = sh
            w2_ref[h, li] = jnp.exp(sh - r2[0][h:h + 1, :])
        return carry

    lax.fori_loop(0, nl, lane_group, 0)


def _peer_scores_call(h2, wq, a1, a2, k2, tt):
    N, D = h2.shape
    H, K, half = k2.shape
    LG = LANES
    nl = tt // LG
    const = lambda shape: pl.BlockSpec(shape, lambda i: (0,) * len(shape))
    row_spec = pl.BlockSpec((nl, K, 2, H, LG), lambda i: (i, 0, 0, 0, 0))
    row_shape = jax.ShapeDtypeStruct((N // LG, K, 2, H, LG), F32)
    key_spec = pl.BlockSpec((H, nl, K, LG), lambda i: (0, i, 0, 0))
    key_shape = jax.ShapeDtypeStruct((H, N // LG, K, LG), F32)
    return pl.pallas_call(
        _peer_scores_kernel,
        grid=(N // tt,),
        in_specs=[pl.BlockSpec((tt, D), lambda i: (i, 0)), const(wq.shape), const(a1.shape), const(a2.shape),
                  const(k2.shape)],
        out_specs=[row_spec, key_spec, key_spec],
        out_shape=[row_shape, key_shape, key_shape],
        scratch_shapes=[pltpu.VMEM((nl, K * H, LG), F32), pltpu.VMEM((nl, K * H, LG), F32),
                        pltpu.VMEM((H, nl, K, LG), F32)],
        compiler_params=_params("parallel"),
        name="peer_scores",
    )(h2, wq, a1, a2, k2)


def _peer_dense_kernel(h2t_ref, te_hbm, s2_ref, w2_ref, u_hbm, v_hbm, x1_ref, mod_ref, fg_ref,
                       o_ref, acc_ref, p0_ref, p1_ref, c0_ref, c1_ref, te0_ref, te1_ref, u0_ref, u1_ref,
                       v0_ref, v1_ref, s2s_ref, w2s_ref, h2ts_ref, sem_te, sem_u, sem_v,
                       *, final, nb, total, layer):
    g = pl.program_id(0)
    tt = h2t_ref.shape[1]
    eb = u0_ref.shape[0]
    nl, rows = te0_ref.shape[0], te0_ref.shape[1]
    K = PEER_KEYS
    LG = LANES
    p_buf, c_buf = (p0_ref, p1_ref), (c0_ref, c1_ref)
    te_buf, u_buf, v_buf = (te0_ref, te1_ref), (u0_ref, u1_ref), (v0_ref, v1_ref)

    def u_copy(k, slot):
        return pltpu.make_async_copy(u_hbm.at[layer, pl.ds((k % nb) * eb, eb), :], u_buf[slot], sem_u.at[slot])

    def v_copy(k, slot):
        return pltpu.make_async_copy(v_hbm.at[layer, pl.ds((k % nb) * eb, eb), :], v_buf[slot], sem_v.at[slot])

    def te_copy(k, slot):
        src = te_hbm.at[pl.ds((k // nb) * nl, nl), pl.ds((k % nb) * rows, rows)]
        return pltpu.make_async_copy(src, te_buf[slot], sem_te.at[slot])

    @pl.when(g == 0)
    def _():
        acc_ref[...] = jnp.zeros_like(acc_ref)
        p1_ref[...] = jnp.zeros_like(p1_ref)
        c0_ref[...] = jnp.zeros_like(c0_ref)
        te1_ref[...] = jnp.zeros_like(te1_ref)
        v0_ref[...] = jnp.zeros_like(v0_ref)
        v1_ref[...] = jnp.zeros_like(v1_ref)
        u_copy(0, 0).start()

    @pl.when(g % nb == 0)
    def _():
        h2ts_ref[...] = h2t_ref[...]

    @pl.when(jnp.logical_or(g == 0, (g - 1) % nb == 0))
    def _():
        s2s_ref[...] = s2_ref[...]
        w2s_ref[...] = w2_ref[...]

    def stages(p_new, p_old, c_new, c_old, u_ref, tes_ref, v_ref):
        nrow = eb // K
        dc = acc_ref.shape[0] // nrow
        QR = 32
        nq = K // QR
        units = [(li, q) for li in range(tt // LG) for q in range(nq)]
        assert len(units) == 2 * nrow
        for ui, (li, q) in enumerate(units):
            piece = ui // 2
            if ui % 2 == 0:
                rs = slice(piece * K, (piece + 1) * K)
                pre = jnp.dot(u_ref[rs, :], h2ts_ref[...], preferred_element_type=F32)
                for lj in range(tt // LG):
                    p_new[lj, rs, :] = pre[:, lj * LG:(lj + 1) * LG]
            else:
                ds = slice(piece * dc, (piece + 1) * dc)
                acc_ref[ds, :] += lax.dot_general(v_ref[:, ds], c_old[...], (((0,), (0,)), ((), ())),
                                                  preferred_element_type=F32)
            ls = slice(li * LG, (li + 1) * LG)
            qs = slice(q * QR, (q + 1) * QR)
            gate = [None] * nrow
            for h in range(PEER_HEADS):
                s2 = s2s_ref[h, li, qs, :]
                w2 = w2s_ref[h, li, qs, :]
                for ri in range(nrow):
                    term = jnp.where(s2 > tes_ref[li, ri, 0, h:h + 1, :], w2, 0.0) * tes_ref[li, ri, 1, h:h + 1, :]
                    gate[ri] = term if gate[ri] is None else gate[ri] + term
            for ri in range(nrow):
                p = p_old[li, ri * K + q * QR:ri * K + (q + 1) * QR, :]
                coef = gate[ri] * (p * (1.0 + lax.erf(p)))
                c_new[ri * K + q * QR:ri * K + (q + 1) * QR, ls] = coef.astype(BF16)

    def step(p):
        q = 1 - p
        @pl.when(g + 1 < total)
        def _():
            u_copy(g + 1, q).start()

        @pl.when(g < total)
        def _():
            te_copy(g, p).start()
            u_copy(g, p).wait()

        @pl.when(jnp.logical_and(g >= 1, g <= total))
        def _():
            v_copy(g - 1, q).start()
            te_copy(g - 1, q).wait()

        @pl.when(g >= 2)
        def _():
            v_copy(g - 2, p).wait()

        stages(p_buf[p], p_buf[q], c_buf[q], c_buf[p], u_buf[p], te_buf[q], v_buf[p])

    @pl.when(g % 2 == 0)
    def _():
        step(0)

    @pl.when(g % 2 == 1)
    def _():
        step(1)

    @pl.when(jnp.logical_and(g >= 2, (g - 2) % nb == nb - 1))
    def _():
        x2 = x1_ref[...] + mod_ref[0, 5:6, :] * acc_ref[...].T
        if final:
            x2 = _rms(x2, fg_ref[...])
        o_ref[...] = x2
        acc_ref[...] = jnp.zeros_like(acc_ref)


def _peer_dense_call(h2t, te, s2, w2, u, v, x1, mod, fg, tt, eb, tokens_per_batch, layer, final):
    ntile, D, tile = h2t.shape
    assert tile == tt
    N = ntile * tt
    H, _, K, LG = s2.shape
    E = u.shape[1]
    nb = E // eb
    total = (N // tt) * nb
    tpb = tokens_per_batch // tt
    rows = eb // K
    blk = lambda g, lag: jnp.clip(g - lag, 0, total - 1)
    hbm = pl.BlockSpec(memory_space=pl.ANY)
    key_spec = pl.BlockSpec((H, tt // LG, K, LG), lambda g: (0, blk(g, 1) // nb, 0, 0))
    return pl.pallas_call(
        functools.partial(_peer_dense_kernel, final=final, nb=nb, total=total, layer=layer),
        grid=(total + 2,),
        in_specs=[
            pl.BlockSpec((None, D, tt), lambda g: (blk(g, 0) // nb, 0, 0)),
            hbm, key_spec, key_spec, hbm, hbm,
            pl.BlockSpec((tt, D), lambda g: (blk(g, 2) // nb, 0)),
            pl.BlockSpec((1, N_MOD, D), lambda g: (blk(g, 2) // nb // tpb, 0, 0)),
            pl.BlockSpec((1, D), lambda g: (0, 0)),
        ],
        out_specs=pl.BlockSpec((tt, D), lambda g: (blk(g, 2) // nb, 0)),
        out_shape=jax.ShapeDtypeStruct((N, D), F32),
        scratch_shapes=[pltpu.VMEM((D, tt), F32),
                        pltpu.VMEM((tt // LG, eb, LG), F32), pltpu.VMEM((tt // LG, eb, LG), F32),
                        pltpu.VMEM((eb, tt), BF16), pltpu.VMEM((eb, tt), BF16),
                        pltpu.VMEM((tt // LG, rows, 2, H, LG), F32), pltpu.VMEM((tt // LG, rows, 2, H, LG), F32),
                        pltpu.VMEM((eb, D), BF16), pltpu.VMEM((eb, D), BF16),
                        pltpu.VMEM((eb, D), BF16), pltpu.VMEM((eb, D), BF16),
                        pltpu.VMEM((H, tt // LG, K, LG), F32), pltpu.VMEM((H, tt // LG, K, LG), F32),
                        pltpu.VMEM((D, tt), BF16),
                        pltpu.SemaphoreType.DMA((2,)), pltpu.SemaphoreType.DMA((2,)), pltpu.SemaphoreType.DMA((2,))],
        compiler_params=_params("arbitrary"),
        name="peer_dense",
    )(h2t, te, s2, w2, u, v, x1, mod, fg)


def _dft_tables(n):
    k = np.arange(n, dtype=np.int64)
    ang = 2.0 * np.pi * ((k[:, None] * k[None, :]) % n) / n
    return np.cos(ang), np.sin(ang)


def _seq_dft_mats(S, scale):
    R = 1 << (int(math.log2(S)) // 2)
    k = jnp.arange(S, dtype=jnp.int32)[None, :]
    w = 2.0 * math.pi / S
    ang_a = (((jnp.arange(S // R, dtype=jnp.int32)[:, None] * R) * k) % S).astype(F32) * w
    ang_b = ((jnp.arange(R, dtype=jnp.int32)[:, None] * k) % S).astype(F32) * w
    ca, sa = jnp.cos(ang_a)[:, None, :], jnp.sin(ang_a)[:, None, :]
    cb, sb = jnp.cos(ang_b)[None, :, :], jnp.sin(ang_b)[None, :, :]
    cos_w = (ca * cb - sa * sb).reshape(S, S)
    sin_w = (sa * cb + ca * sb).reshape(S, S)
    return (cos_w * scale).astype(BF16), (sin_w * (-scale)).astype(BF16)


def _block_diag(blocks):
    G, a, b = blocks.shape
    eye = jnp.eye(G, dtype=blocks.dtype)
    return (eye[:, None, :, None] * blocks[:, :, None, :]).reshape(G * a, G * b)


def _key_head_rows(keys):
    H, K, d = keys.shape
    eye = jnp.eye(H, dtype=keys.dtype)
    return (jnp.transpose(keys, (1, 0, 2))[:, :, None, :] * eye[None, :, :, None]).reshape(K * H, H * d).astype(BF16)


def kernel(x, c, mod_w, mod_b, norm1_g, w_in, gmlp_ws, gmlp_bs, fnet_w, fnet_b, gain_a, gain_b, w_out,
           norm2_g, peer_wq, peer_k1, peer_k2, peer_u, peer_v, final_g):
    B, S, D = x.shape
    L = mod_w.shape[0]
    wa = gain_a.shape[1]
    wf = gain_b.shape[1]
    hd = wa // GMLP_HEADS
    gd = wf // FNET_GROUPS
    N = B * S
    ts = min(TOKEN_TILE, S)
    tt = min(TOKEN_TILE, S)
    eb = 8 * PEER_KEYS

    mod = _mod_call(c, mod_w, mod_b).reshape(L, B, N_MOD, D)

    avg = _block_diag(jnp.full((GMLP_HEADS, hd, hd), 1.0 / hd, F32)).astype(BF16)
    hmask = _block_diag(jnp.ones((GMLP_HEADS, CHUNK, hd), F32)).astype(BF16)
    cos_d, sin_d = _dft_tables(gd)
    cd = _block_diag(jnp.asarray(np.broadcast_to(cos_d, (FNET_GROUPS, gd, gd)), F32)).astype(BF16)
    sd = _block_diag(jnp.asarray(np.broadcast_to(sin_d, (FNET_GROUPS, gd, gd)), F32)).astype(BF16)
    wc, wsn = _seq_dft_mats(S, 1.0 / math.sqrt(S * gd))
    u_all = (peer_u * math.sqrt(0.5)).astype(BF16)
    v_all = peer_v.astype(BF16)

    for l in range(L):
        ws_cat = jnp.transpose(gmlp_ws[l], (1, 0, 2)).reshape(CHUNK, GMLP_HEADS * CHUNK).astype(BF16)
        bs_full = jnp.repeat(gmlp_bs[l].T, hd, axis=1)
        ya, fc, fs = _mixer_in_call(x, mod[l], norm1_g[l][None], w_in[l].astype(BF16), avg, ws_cat, bs_full,
                                    hmask, cd, sd, ts)
        fr = _seq_dft_call(wc, wsn, fc, fs, ts)
        x1, h2, h2t = _mixer_out_call(x, ya, fr, mod[l], _block_diag(fnet_w[l]).astype(BF16),
                                 fnet_b[l].reshape(1, wf), gain_a[l][None], gain_b[l][None],
                                 w_out[l].astype(BF16), norm2_g[l][None], ts)
        h2 = h2.reshape(N, D)
        te, s2, w2 = _peer_scores_call(h2, peer_wq[l].astype(BF16), _key_head_rows(peer_k1[l]),
                                           _key_head_rows(peer_k2[l]), peer_k2[l].astype(BF16), tt)
        x = _peer_dense_call(h2t, te, s2, w2, u_all, v_all, x1.reshape(N, D), mod[l], final_g[None],
                             tt, eb, S, l, l == L - 1).reshape(B, S, D)
    return x
```

```python
import functools
import math

import numpy as np
import jax
import jax.numpy as jnp
from jax import lax
from jax.experimental import pallas as pl
from jax.experimental.pallas import tpu as pltpu

GMLP_HEADS = 8
FNET_GROUPS = 8
CHUNK = 128
PEER_HEADS = 8
PEER_KEYS = 128
PEER_TOPK = 16
N_MOD = 6
EPS = 1e-6

LANES = 128
VMEM_LIMIT_BYTES = 56 * 1024 * 1024
TOKEN_TILE = 512
MOD_COLS = 1536

F32 = jnp.float32
BF16 = jnp.bfloat16
NEG_INF = float("-inf")


def _params(*sem):
    return pltpu.CompilerParams(dimension_semantics=sem, vmem_limit_bytes=VMEM_LIMIT_BYTES)


def _gelu(x):
    return 0.5 * x * (1.0 + lax.erf(x * (1.0 / math.sqrt(2.0))))


def _rms(x, g):
    return x * lax.rsqrt(jnp.mean(x * x, axis=-1, keepdims=True) + EPS) * g


def _mod_kernel(c_ref, w_ref, b_ref, o_ref):
    c = c_ref[...]
    sc = c * jax.nn.sigmoid(c)
    o_ref[0] = jnp.dot(sc, w_ref[0], preferred_element_type=F32) + b_ref[0]


def _mod_call(c, mod_w, mod_b):
    L, D, M = mod_w.shape
    B = c.shape[0]
    tn = MOD_COLS
    return pl.pallas_call(
        _mod_kernel,
        grid=(L, M // tn),
        in_specs=[
            pl.BlockSpec((B, D), lambda l, j: (0, 0)),
            pl.BlockSpec((1, D, tn), lambda l, j: (l, 0, j)),
            pl.BlockSpec((1, 1, tn), lambda l, j: (l, 0, j)),
        ],
        out_specs=pl.BlockSpec((1, B, tn), lambda l, j: (l, 0, j)),
        out_shape=jax.ShapeDtypeStruct((L, B, M), F32),
        compiler_params=_params("parallel", "parallel"),
        name="adaln_mod",
    )(c, mod_w, mod_b.reshape(L, 1, M))


def _mixer_in_kernel(x_ref, mod_ref, g_ref, win_ref, avg_ref, ws_ref, bs_ref, hmask_ref, cd_ref, sd_ref,
                     ya_ref, fc_ref, fs_ref):
    ts = x_ref.shape[1]
    wa = ya_ref.shape[2]
    x = x_ref[0]
    h = _rms(x, g_ref[...]) * (1.0 + mod_ref[0, 1:2, :]) + mod_ref[0, 0:1, :]
    z = jnp.dot(h.astype(BF16), win_ref[...], preferred_element_type=F32)
    u = _gelu(z[:, :wa])
    gv = _gelu(z[:, wa:2 * wa])
    zf = z[:, 2 * wa:].astype(BF16)
    mu = jnp.dot(gv.astype(BF16), avg_ref[...], preferred_element_type=F32)
    dv = gv - mu
    var = jnp.dot((dv * dv).astype(BF16), avg_ref[...], preferred_element_type=F32)
    v = (dv * lax.rsqrt(var + EPS)).astype(BF16)
    hmask = hmask_ref[...]
    for ci in range(ts // CHUNK):
        vc = v[ci * CHUNK:(ci + 1) * CHUNK, :]
        vstack = jnp.concatenate([vc] * GMLP_HEADS, axis=0) * hmask
        sv = jnp.dot(ws_ref[...], vstack, preferred_element_type=F32) + bs_ref[...]
        ya_ref[0, ci * CHUNK:(ci + 1) * CHUNK, :] = u[ci * CHUNK:(ci + 1) * CHUNK, :] * sv
    fc_ref[...] = jnp.dot(zf, cd_ref[...], preferred_element_type=F32).astype(BF16)
    fs_ref[...] = jnp.dot(zf, sd_ref[...], preferred_element_type=F32).astype(BF16)


def _mixer_in_call(x, mod, g, win, avg, ws_cat, bs_full, hmask, cd, sd, ts):
    B, S, D = x.shape
    wa = avg.shape[0]
    wf = cd.shape[0]
    const = lambda shape: pl.BlockSpec(shape, lambda b, s: (0,) * len(shape))
    return pl.pallas_call(
        _mixer_in_kernel,
        grid=(B, S // ts),
        in_specs=[
            pl.BlockSpec((1, ts, D), lambda b, s: (b, s, 0)),
            pl.BlockSpec((1, N_MOD, D), lambda b, s: (b, 0, 0)),
            const(g.shape), const(win.shape), const(avg.shape), const(ws_cat.shape),
            const(bs_full.shape), const(hmask.shape), const(cd.shape), const(sd.shape),
        ],
        out_specs=[
            pl.BlockSpec((1, ts, wa), lambda b, s: (b, s, 0)),
            pl.BlockSpec((None, ts, wf), lambda b, s: (b, s, 0)),
            pl.BlockSpec((None, ts, wf), lambda b, s: (b, s, 0)),
        ],
        out_shape=[
            jax.ShapeDtypeStruct((B, S, wa), F32),
            jax.ShapeDtypeStruct((B, S, wf), BF16),
            jax.ShapeDtypeStruct((B, S, wf), BF16),
        ],
        compiler_params=_params("parallel", "parallel"),
        name="mixer_in",
    )(x, mod, g, win, avg, ws_cat, bs_full, hmask, cd, sd)


def _seq_dft_kernel(wc_ref, wsn_ref, fc_ref, fs_ref, o_ref):
    acc = jnp.dot(wc_ref[...], fc_ref[...], preferred_element_type=F32)
    acc += jnp.dot(wsn_ref[...], fs_ref[...], preferred_element_type=F32)
    o_ref[...] = acc.astype(o_ref.dtype)


def _seq_dft_call(wc, wsn, fc, fs, tm):
    B, S, wf = fc.shape
    return pl.pallas_call(
        _seq_dft_kernel,
        grid=(S // tm, B),
        in_specs=[
            pl.BlockSpec((tm, S), lambda i, j: (i, 0)),
            pl.BlockSpec((tm, S), lambda i, j: (i, 0)),
            pl.BlockSpec((None, S, wf), lambda i, j: (j, 0, 0)),
            pl.BlockSpec((None, S, wf), lambda i, j: (j, 0, 0)),
        ],
        out_specs=pl.BlockSpec((None, tm, wf), lambda i, j: (j, i, 0)),
        out_shape=jax.ShapeDtypeStruct((B, S, wf), BF16),
        compiler_params=_params("parallel", "parallel"),
        name="seq_dft",
    )(wc, wsn, fc, fs)


def _mixer_out_kernel(x_ref, ya_ref, fr_ref, mod_ref, fw_ref, fb_ref, ga_ref, gb_ref, wout_ref, g2_ref,
                      x1_ref, h2_ref, h2t_ref):
    wa = ya_ref.shape[2]
    yb = jnp.dot(fr_ref[...], fw_ref[...], preferred_element_type=F32) + fb_ref[...]
    na = _rms(ya_ref[0], ga_ref[...]).astype(BF16)
    nb = _rms(yb, gb_ref[...]).astype(BF16)
    mix = jnp.dot(na, wout_ref[:wa, :], preferred_element_type=F32)
    mix += jnp.dot(nb, wout_ref[wa:, :], preferred_element_type=F32)
    x1 = x_ref[0] + mod_ref[0, 2:3, :] * mix
    x1_ref[0] = x1
    h2 = _rms(x1, g2_ref[...]) * (1.0 + mod_ref[0, 4:5, :]) + mod_ref[0, 3:4, :]
    h2_ref[0] = h2.astype(BF16)
    h2t_ref[...] = h2.T.astype(BF16)


def _mixer_out_call(x, ya, fr, mod, fw_bd, fb, ga, gb, wout, g2, ts):
    B, S, D = x.shape
    wa = ya.shape[2]
    wf = fw_bd.shape[0]
    const = lambda shape: pl.BlockSpec(shape, lambda b, s: (0,) * len(shape))
    return pl.pallas_call(
        _mixer_out_kernel,
        grid=(B, S // ts),
        in_specs=[
            pl.BlockSpec((1, ts, D), lambda b, s: (b, s, 0)),
            pl.BlockSpec((1, ts, wa), lambda b, s: (b, s, 0)),
            pl.BlockSpec((None, ts, wf), lambda b, s: (b, s, 0)),
            pl.BlockSpec((1, N_MOD, D), lambda b, s: (b, 0, 0)),
            const(fw_bd.shape), const(fb.shape), const(ga.shape), const(gb.shape),
            const(wout.shape), const(g2.shape),
        ],
        out_specs=[
            pl.BlockSpec((1, ts, D), lambda b, s: (b, s, 0)),
            pl.BlockSpec((1, ts, D), lambda b, s: (b, s, 0)),
            pl.BlockSpec((None, D, ts), lambda b, s: (b * (S // ts) + s, 0, 0)),
        ],
        out_shape=[
            jax.ShapeDtypeStruct((B, S, D), F32),
            jax.ShapeDtypeStruct((B, S, D), BF16),
            jax.ShapeDtypeStruct((B * S // ts, D, ts), BF16),
        ],
        compiler_params=_params("parallel", "parallel"),
        name="mixer_out",
    )(x, ya, fr, mod, fw_bd, fb, ga, gb, wout, g2)


def _sort16_pairs():
    pairs = []

    def merge(lo, n, r):
        m = 2 * r
        if m < n:
            merge(lo, n, m)
            merge(lo + r, n, m)
            pairs.extend((i, i + r) for i in range(lo + r, lo + n - r, m))
        else:
            pairs.append((lo, lo + r))

    def sort(lo, n):
        if n > 1:
            sort(lo, n // 2)
            sort(lo + n // 2, n // 2)
            merge(lo, n, 1)

    sort(0, 16)
    return pairs


_SORT16 = _sort16_pairs()
_BITONIC16 = [(i, i + s) for s in (8, 4, 2, 1) for i in range(16) if (i // s) % 2 == 0]


def _exchange(v, pairs):
    v = list(v)
    for i, j in pairs:
        v[i], v[j] = jnp.maximum(v[i], v[j]), jnp.minimum(v[i], v[j])
    return v


def _top16(vals):
    groups = [(_exchange(vals[g:g + 16], _SORT16), None) for g in range(0, len(vals), 16)]
    while len(groups) > 1:
        merged = []
        for (a, da), (b, db) in zip(groups[0::2], groups[1::2]):
            keep = [jnp.maximum(a[i], b[15 - i]) for i in range(16)]
            drop = [jnp.minimum(a[i], b[15 - i]) for i in range(16)] + [d for d in (da, db) if d is not None]
            while len(drop) > 1:
                drop = [jnp.maximum(x, y) for x, y in zip(drop[0::2], drop[1::2])] + drop[len(drop) & ~1:]
            merged.append((_exchange(keep, _BITONIC16), drop[0]))
        groups = merged
    return groups[0]


def _peer_scores_kernel(h2_ref, wq_ref, a1_ref, a2_ref, k2_ref, te_ref, s2_ref, w2_ref,
                        s1_scr, s2_scr, s2h_scr):
    tt = h2_ref.shape[0]
    H, K, half = k2_ref.shape
    LG = LANES
    nl = tt // LG
    nt = (((1,), (1,)), ((), ()))
    q = jnp.dot(h2_ref[...], wq_ref[...], preferred_element_type=F32).astype(BF16)
    q1 = jnp.concatenate([q[:, (2 * h) * half:(2 * h + 1) * half] for h in range(H)], axis=1)
    q2 = jnp.concatenate([q[:, (2 * h + 1) * half:(2 * h + 2) * half] for h in range(H)], axis=1)
    s1 = lax.dot_general(a1_ref[...], q1, nt, preferred_element_type=F32)
    s2 = lax.dot_general(a2_ref[...], q2, nt, preferred_element_type=F32)
    for li in range(nl):
        s1_scr[li] = s1[:, li * LG:(li + 1) * LG]
        s2_scr[li] = s2[:, li * LG:(li + 1) * LG]
    for h in range(H):
        sh = lax.dot_general(k2_ref[h], q[:, (2 * h + 1) * half:(2 * h + 2) * half], nt,
                             preferred_element_type=F32)
        for li in range(nl):
            s2h_scr[h, li] = sh[:, li * LG:(li + 1) * LG]

    def lane_group(li, carry):
        v1 = [s1_scr[li, k * H:(k + 1) * H, :] for k in range(K)]
        v2 = [s2_scr[li, k * H:(k + 1) * H, :] for k in range(K)]
        r1, d1 = _top16(v1)
        r2, d2 = _top16(v2)
        r1, r2 = r1 + [d1], r2 + [d2]
        cand = [r1[j] + r2[l] for j in range(17) for l in range(17) if (j + 1) * (l + 1) <= 17]
        pad = [jnp.full((H, LG), NEG_INF, F32)] * (-len(cand) % 16)
        top, c17 = _top16(cand + pad)
        mid = 0.5 * (top[PEER_TOPK - 1] + c17)
        m = r1[0] + r2[0]
        z = None
        for c in cand:
            term = jnp.where(c > mid, jnp.exp(c - m), 0.0)
            z = term if z is None else z + term
        scale = math.sqrt(0.5) / z
        for k in range(K):
            te_ref[li, k, 0] = mid - v1[k]
            te_ref[li, k, 1] = jnp.exp(v1[k] - r1[0]) * scale
        for h in range(H):
            sh = s2h_scr[h, li]
            s2_ref[h, li] = sh
            w2_ref[h, li] = jnp.exp(sh - r2[0][h:h + 1, :])
        return carry

    lax.fori_loop(0, nl, lane_group, 0)


def _peer_scores_call(h2, wq, a1, a2, k2, tt):
    N, D = h2.shape
    H, K, half = k2.shape
    LG = LANES
    nl = tt // LG
    const = lambda shape: pl.BlockSpec(shape, lambda i: (0,) * len(shape))
    row_spec = pl.BlockSpec((nl, K, 2, H, LG), lambda i: (i, 0, 0, 0, 0))
    row_shape = jax.ShapeDtypeStruct((N // LG, K, 2, H, LG), F32)
    key_spec = pl.BlockSpec((H, nl, K, LG), lambda i: (0, i, 0, 0))
    key_shape = jax.ShapeDtypeStruct((H, N // LG, K, LG), F32)
    return pl.pallas_call(
        _peer_scores_kernel,
        grid=(N // tt,),
        in_specs=[pl.BlockSpec((tt, D), lambda i: (i, 0)), const(wq.shape), const(a1.shape), const(a2.shape),
                  const(k2.shape)],
        out_specs=[row_spec, key_spec, key_spec],
        out_shape=[row_shape, key_shape, key_shape],
        scratch_shapes=[pltpu.VMEM((nl, K * H, LG), F32), pltpu.VMEM((nl, K * H, LG), F32),
                        pltpu.VMEM((H, nl, K, LG), F32)],
        compiler_params=_params("parallel"),
        name="peer_scores",
    )(h2, wq, a1, a2, k2)


def _peer_dense_kernel(h2t_ref, te_ref, s2_ref, w2_ref, u_ref, v_ref, x1_ref, mod_ref, fg_ref,
                       o_ref, acc_ref, p0_ref, p1_ref, c0_ref, c1_ref,
                       tes_ref, s2s_ref, w2s_ref, h2ts_ref, *, final, nb):
    g = pl.program_id(0)
    tt = h2t_ref.shape[1]
    eb = u_ref.shape[0]
    K = PEER_KEYS
    LG = LANES

    @pl.when(g == 0)
    def _():
        acc_ref[...] = jnp.zeros_like(acc_ref)
        p1_ref[...] = jnp.zeros_like(p1_ref)
        c0_ref[...] = jnp.zeros_like(c0_ref)

    tes_ref[...] = te_ref[...]

    @pl.when(g % nb == 0)
    def _():
        h2ts_ref[...] = h2t_ref[...]

    @pl.when(jnp.logical_or(g == 0, (g - 1) % nb == 0))
    def _():
        s2s_ref[...] = s2_ref[...]
        w2s_ref[...] = w2_ref[...]

    def stages(p_new, p_old, c_new, c_old):
        nrow = eb // K
        dc = acc_ref.shape[0] // nrow
        QR = 32
        nq = K // QR
        units = [(li, q) for li in range(tt // LG) for q in range(nq)]
        assert len(units) == 2 * nrow
        for ui, (li, q) in enumerate(units):
            piece = ui // 2
            if ui % 2 == 0:
                rs = slice(piece * K, (piece + 1) * K)
                pre = jnp.dot(u_ref[rs, :], h2ts_ref[...], preferred_element_type=F32)
                for lj in range(tt // LG):
                    p_new[lj, rs, :] = pre[:, lj * LG:(lj + 1) * LG]
            else:
                ds = slice(piece * dc, (piece + 1) * dc)
                acc_ref[ds, :] += lax.dot_general(v_ref[:, ds], c_old[...], (((0,), (0,)), ((), ())),
                                                  preferred_element_type=F32)
            ls = slice(li * LG, (li + 1) * LG)
            qs = slice(q * QR, (q + 1) * QR)
            gate = [None] * nrow
            for h in range(PEER_HEADS):
                s2 = s2s_ref[h, li, qs, :]
                w2 = w2s_ref[h, li, qs, :]
                for ri in range(nrow):
                    term = jnp.where(s2 > tes_ref[li, ri, 0, h:h + 1, :], w2, 0.0) * tes_ref[li, ri, 1, h:h + 1, :]
                    gate[ri] = term if gate[ri] is None else gate[ri] + term
            for ri in range(nrow):
                p = p_old[li, ri * K + q * QR:ri * K + (q + 1) * QR, :]
                coef = gate[ri] * (p * (1.0 + lax.erf(p)))
                c_new[ri * K + q * QR:ri * K + (q + 1) * QR, ls] = coef.astype(BF16)

    @pl.when(g % 2 == 0)
    def _():
        stages(p0_ref, p1_ref, c1_ref, c0_ref)

    @pl.when(g % 2 == 1)
    def _():
        stages(p1_ref, p0_ref, c0_ref, c1_ref)

    @pl.when(jnp.logical_and(g >= 2, (g - 2) % nb == nb - 1))
    def _():
        x2 = x1_ref[...] + mod_ref[0, 5:6, :] * acc_ref[...].T
        if final:
            x2 = _rms(x2, fg_ref[...])
        o_ref[...] = x2
        acc_ref[...] = jnp.zeros_like(acc_ref)


def _peer_dense_call(h2t, te, s2, w2, u, v, x1, mod, fg, tt, eb, tokens_per_batch, layer, final):
    ntile, D, tile = h2t.shape
    assert tile == tt
    N = ntile * tt
    H, _, K, LG = s2.shape
    E = u.shape[1]
    nb = E // eb
    total = (N // tt) * nb
    tpb = tokens_per_batch // tt
    rows = eb // K
    blk = lambda g, lag: jnp.clip(g - lag, 0, total - 1)
    row_spec = pl.BlockSpec((tt // LG, rows, 2, H, LG), lambda g: (blk(g, 1) // nb, blk(g, 1) % nb, 0, 0, 0))
    key_spec = pl.BlockSpec((H, tt // LG, K, LG), lambda g: (0, blk(g, 1) // nb, 0, 0),
                            pipeline_mode=pl.Buffered(1))
    return pl.pallas_call(
        functools.partial(_peer_dense_kernel, final=final, nb=nb),
        grid=(total + 2,),
        in_specs=[
            pl.BlockSpec((None, D, tt), lambda g: (blk(g, 0) // nb, 0, 0), pipeline_mode=pl.Buffered(1)),
            row_spec, key_spec, key_spec,
            pl.BlockSpec((None, eb, D), lambda g: (layer, blk(g, 0) % nb, 0)),
            pl.BlockSpec((None, eb, D), lambda g: (layer, blk(g, 2) % nb, 0)),
            pl.BlockSpec((tt, D), lambda g: (blk(g, 2) // nb, 0)),
            pl.BlockSpec((1, N_MOD, D), lambda g: (blk(g, 2) // nb // tpb, 0, 0)),
            pl.BlockSpec((1, D), lambda g: (0, 0)),
        ],
        out_specs=pl.BlockSpec((tt, D), lambda g: (blk(g, 2) // nb, 0)),
        out_shape=jax.ShapeDtypeStruct((N, D), F32),
        scratch_shapes=[pltpu.VMEM((D, tt), F32),
                        pltpu.VMEM((tt // LG, eb, LG), F32), pltpu.VMEM((tt // LG, eb, LG), F32),
                        pltpu.VMEM((eb, tt), BF16), pltpu.VMEM((eb, tt), BF16),
                        pltpu.VMEM((tt // LG, rows, 2, H, LG), F32),
                        pltpu.VMEM((H, tt // LG, K, LG), F32), pltpu.VMEM((H, tt // LG, K, LG), F32),
                        pltpu.VMEM((D, tt), BF16)],
        compiler_params=_params("arbitrary"),
        name="peer_dense",
    )(h2t, te, s2, w2, u, v, x1, mod, fg)


def _dft_tables(n):
    k = np.arange(n, dtype=np.int64)
    ang = 2.0 * np.pi * ((k[:, None] * k[None, :]) % n) / n
    return np.cos(ang), np.sin(ang)


def _seq_dft_mats(S, scale):
    R = 1 << (int(math.log2(S)) // 2)
    k = jnp.arange(S, dtype=jnp.int32)[None, :]
    w = 2.0 * math.pi / S
    ang_a = (((jnp.arange(S // R, dtype=jnp.int32)[:, None] * R) * k) % S).astype(F32) * w
    ang_b = ((jnp.arange(R, dtype=jnp.int32)[:, None] * k) % S).astype(F32) * w
    ca, sa = jnp.cos(ang_a)[:, None, :], jnp.sin(ang_a)[:, None, :]
    cb, sb = jnp.cos(ang_b)[None, :, :], jnp.sin(ang_b)[None, :, :]
    cos_w = (ca * cb - sa * sb).reshape(S, S)
    sin_w = (sa * cb + ca * sb).reshape(S, S)
    return (cos_w * scale).astype(BF16), (sin_w * (-scale)).astype(BF16)


def _block_diag(blocks):
    G, a, b = blocks.shape
    eye = jnp.eye(G, dtype=blocks.dtype)
    return (eye[:, None, :, None] * blocks[:, :, None, :]).reshape(G * a, G * b)


def _key_head_rows(keys):
    H, K, d = keys.shape
    eye = jnp.eye(H, dtype=keys.dtype)
    return (jnp.transpose(keys, (1, 0, 2))[:, :, None, :] * eye[None, :, :, None]).reshape(K * H, H * d).astype(BF16)


def kernel(x, c, mod_w, mod_b, norm1_g, w_in, gmlp_ws, gmlp_bs, fnet_w, fnet_b, gain_a, gain_b, w_out,
           norm2_g, peer_wq, peer_k1, peer_k2, peer_u, peer_v, final_g):
    B, S, D = x.shape
    L = mod_w.shape[0]
    wa = gain_a.shape[1]
    wf = gain_b.shape[1]
    hd = wa // GMLP_HEADS
    gd = wf // FNET_GROUPS
    N = B * S
    ts = min(TOKEN_TILE, S)
    tt = min(TOKEN_TILE, S)
    eb = 8 * PEER_KEYS

    mod = _mod_call(c, mod_w, mod_b).reshape(L, B, N_MOD, D)

    avg = _block_diag(jnp.full((GMLP_HEADS, hd, hd), 1.0 / hd, F32)).astype(BF16)
    hmask = _block_diag(jnp.ones((GMLP_HEADS, CHUNK, hd), F32)).astype(BF16)
    cos_d, sin_d = _dft_tables(gd)
    cd = _block_diag(jnp.asarray(np.broadcast_to(cos_d, (FNET_GROUPS, gd, gd)), F32)).astype(BF16)
    sd = _block_diag(jnp.asarray(np.broadcast_to(sin_d, (FNET_GROUPS, gd, gd)), F32)).astype(BF16)
    wc, wsn = _seq_dft_mats(S, 1.0 / math.sqrt(S * gd))
    u_all = (peer_u * math.sqrt(0.5)).astype(BF16)
    v_all = peer_v.astype(BF16)

    for l in range(L):
        ws_cat = jnp.transpose(gmlp_ws[l], (1, 0, 2)).reshape(CHUNK, GMLP_HEADS * CHUNK).astype(BF16)
        bs_full = jnp.repeat(gmlp_bs[l].T, hd, axis=1)
        ya, fc, fs = _mixer_in_call(x, mod[l], norm1_g[l][None], w_in[l].astype(BF16), avg, ws_cat, bs_full,
                                    hmask, cd, sd, ts)
        fr = _seq_dft_call(wc, wsn, fc, fs, ts)
        x1, h2, h2t = _mixer_out_call(x, ya, fr, mod[l], _block_diag(fnet_w[l]).astype(BF16),
                                 fnet_b[l].reshape(1, wf), gain_a[l][None], gain_b[l][None],
                                 w_out[l].astype(BF16), norm2_g[l][None], ts)
        h2 = h2.reshape(N, D)
        te, s2, w2 = _peer_scores_call(h2, peer_wq[l].astype(BF16), _key_head_rows(peer_k1[l]),
                                           _key_head_rows(peer_k2[l]), peer_k2[l].astype(BF16), tt)
        x = _peer_dense_call(h2t, te, s2, w2, u_all, v_all, x1.reshape(N, D), mod[l], final_g[None],
                             tt, eb, S, l, l == L - 1).reshape(B, S, D)
    return x
```
